```python
import jax, jax.numpy as jnp
from jax import lax
import numpy as np

D_MODEL = 1024
BATCH = 8
SEQ = 2048
DEPTH = 4
DEC_BATCH = 32
DEC_SEQ = 4
PAST_LEN = 8192
PAGE_SIZE = 128

PLE_DIM = 256
HD_FOX = 64
W_FOX = D_MODEL // 2
H_FOX = W_FOX // HD_FOX
H_GLA = 4
WV_GLA = D_MODEL // 2
WK_GLA = WV_GLA // 2
DK_GLA = WK_GLA // H_GLA
DV_GLA = WV_GLA // H_GLA
GLA_RANK = 16
GLA_GATE_NORM = 16.0
GLA_CHUNK = 64
FOX_BLOCK = 128
EPS = 1e-6

_IN_SIZES = (W_FOX, W_FOX, W_FOX, W_FOX, H_FOX, WK_GLA, WK_GLA, WV_GLA, WV_GLA, GLA_RANK, D_MODEL, D_MODEL)
N_IN = sum(_IN_SIZES)
_IN_OFFSETS = tuple(int(o) for o in np.cumsum(_IN_SIZES)[:-1])

kernel_name = "fox_gla_gated_hybrid_step"


def rmsnorm(x, g):
    xf = x.astype(jnp.float32)
    y = xf * lax.rsqrt(jnp.mean(xf * xf, axis=-1, keepdims=True) + EPS)
    return (y * g.astype(jnp.float32)).astype(x.dtype)


def fox_attend(q, k, v, c_q, c_k, q_pos, k_pos):
    s = jnp.einsum('bqhd,bkhd->bhqk', q, k).astype(jnp.float32) * (HD_FOX ** -0.5)
    s = s + (jnp.transpose(c_q, (0, 2, 1))[:, :, :, None] - jnp.transpose(c_k, (0, 2, 1))[:, :, None, :])
    mask = k_pos[None, :] <= q_pos[:, None]
    s = jnp.where(mask, s, -jnp.inf)
    pr = jax.nn.softmax(s, axis=-1)
    return jnp.einsum('bhqk,bkhd->bqhd', pr.astype(v.dtype), v)


def fox_prompt(q, k, v, logf):
    L = q.shape[1]
    c = jnp.cumsum(logf.astype(jnp.float32), axis=1)
    blk = min(FOX_BLOCK, L)
    pos = jnp.arange(L)
    outs = []
    for i in range(L // blk):
        s0, e = i * blk, (i + 1) * blk
        outs.append(fox_attend(q[:, s0:e], k[:, :e], v[:, :e], c[:, s0:e], c[:, :e], pos[s0:e], pos[:e]))
    return jnp.concatenate(outs, axis=1)


def fox_sample(q, k, v, logf, k_past, v_past, logf_past):
    P, L = k_past.shape[1], q.shape[1]
    k_all = jnp.concatenate([k_past, k], axis=1)
    v_all = jnp.concatenate([v_past, v], axis=1)
    lf = jnp.concatenate([logf_past.astype(jnp.float32), logf.astype(jnp.float32)], axis=1)
    c = jnp.cumsum(lf, axis=1)
    pos = jnp.arange(P + L)
    return fox_attend(q, k_all, v_all, c[:, P:], c, pos[P:], pos)


def gla_chunked(q, k, v, log_a, s0):
    b_sz, L, nh, dk = q.shape
    dv = v.shape[-1]
    c = min(GLA_CHUNK, L)
    n = -(-L // c)
    pad = n * c - L

    def chunks(t):
        t = jnp.pad(t.astype(jnp.float32), ((0, 0), (0, pad), (0, 0), (0, 0)))
        return t.reshape(b_sz, n, c, nh, t.shape[-1]).transpose(1, 0, 3, 2, 4)

    qc, kc, vc, gc = chunks(q), chunks(k), chunks(v), chunks(log_a)
    bc = jnp.cumsum(gc, axis=3)
    b_last = bc[:, :, :, -1:, :]
    q_dec = qc * jnp.exp(bc) * (dk ** -0.5)
    k_in = kc * jnp.exp(-bc)
    k_out = kc * jnp.exp(b_last - bc)
    causal = jnp.tril(jnp.ones((c, c), dtype=bool))
    a_intra = jnp.where(causal, jnp.einsum('nbhtd,nbhsd->nbhts', q_dec, k_in), 0.0)
    o_intra = jnp.einsum('nbhts,nbhsv->nbhtv', a_intra, vc)

    def step(s, inp):
        qd, ko, vv, bl = inp
        o = jnp.einsum('bhtd,bhdv->bhtv', qd, s)
        s = s * jnp.exp(bl[:, :, 0, :, None]) + jnp.einsum('bhsd,bhsv->bhdv', ko, vv)
        return s, o

    s_fin, o_inter = lax.scan(step, s0.astype(jnp.float32), (q_dec, k_out, vc, b_last))
    o = (o_intra + o_inter).transpose(1, 0, 3, 2, 4).reshape(b_sz, n * c, nh, dv)[:, :L]
    return o, s_fin


def mixer_layer(x, p_l, lw, past):
    B, L, _ = x.shape
    h = rmsnorm(x, lw['g_pre'])
    u = h @ lw['w_in']
    qf, kf, vf, zf, ff, qg, kg, vg, zg, ag, gate_f, gate_g = jnp.split(u, list(_IN_OFFSETS), axis=-1)
    qf = rmsnorm(qf.reshape(B, L, H_FOX, HD_FOX), lw['g_q'])
    kf = rmsnorm(kf.reshape(B, L, H_FOX, HD_FOX), lw['g_k'])
    vf = vf.reshape(B, L, H_FOX, HD_FOX)
    logf = jax.nn.log_sigmoid((ff + lw['b_fox_f']).astype(jnp.float32))
    if past is None:
        of = fox_prompt(qf, kf, vf, logf)
        s0 = jnp.zeros((B, H_GLA, DK_GLA, DV_GLA), jnp.float32)
    else:
        k_past, v_past, logf_past, s0 = past
        of = fox_sample(qf, kf, vf, logf, k_past, v_past, logf_past)
    of = of.reshape(B, L, W_FOX) * jax.nn.silu(zf)
    log_a = jax.nn.log_sigmoid((ag @ lw['w_gla_a2'] + lw['b_gla_a']).astype(jnp.float32)) / GLA_GATE_NORM
    og, s_new = gla_chunked(qg.reshape(B, L, H_GLA, DK_GLA), kg.reshape(B, L, H_GLA, DK_GLA),
                            vg.reshape(B, L, H_GLA, DV_GLA), log_a.reshape(B, L, H_GLA, DK_GLA), s0)
    og = rmsnorm(og, lw['g_gla_o']).reshape(B, L, WV_GLA).astype(x.dtype) * jax.nn.silu(zg)
    m = (jax.nn.sigmoid(gate_f + lw['b_merge'][0]) * (of @ lw['w_br_fox'])
         + jax.nn.sigmoid(gate_g + lw['b_merge'][1]) * (og @ lw['w_br_gla']))
    x = x + m @ lw['w_out']
    hp = rmsnorm(x, lw['g_ple'])
    x = x + jax.nn.sigmoid(hp @ lw['w_ple_gate'] + lw['b_ple_gate']) * (p_l @ lw['w_ple'])
    return x, kf, vf, logf, s_new


def setup_inputs(seed: int = 0) -> dict:
    key = jax.random.key(seed)
    ks = jax.random.split(key, 32)
    f32 = jnp.float32
    n_pages = PAST_LEN // PAGE_SIZE
    n_used = DEC_BATCH * n_pages
    n_phys = n_used + max(1, n_used // 4)
    perm = jax.random.permutation(ks[0], n_phys)
    page_table = perm[:n_used].reshape(DEC_BATCH, n_pages).astype(jnp.int32)

    def nrm(k, shape, scale=1.0):
        return jax.random.normal(k, shape, f32) * scale

    def gain(k, shape):
        return 1.0 + 0.02 * jax.random.normal(k, shape, f32)

    return {
        'x_prompt': nrm(ks[1], (BATCH, SEQ, D_MODEL)),
        'x_sample': nrm(ks[2], (DEC_BATCH, DEC_SEQ, D_MODEL)),
        'p_prompt': nrm(ks[3], (DEPTH, BATCH, SEQ, PLE_DIM)),
        'p_sample': nrm(ks[4], (DEPTH, DEC_BATCH, DEC_SEQ, PLE_DIM)),
        'cache_k': nrm(ks[5], (DEPTH, n_phys, PAGE_SIZE, H_FOX, HD_FOX)),
        'cache_v': nrm(ks[6], (DEPTH, n_phys, PAGE_SIZE, H_FOX, HD_FOX)),
        'cache_logf': jax.nn.log_sigmoid(2.5 + nrm(ks[7], (DEPTH, n_phys, PAGE_SIZE, H_FOX))),
        'state_gla': nrm(ks[8], (DEPTH, DEC_BATCH, H_GLA, DK_GLA, DV_GLA), 0.3),
        'page_table': page_table,
        'w_in': nrm(ks[9], (DEPTH, D_MODEL, N_IN), D_MODEL ** -0.5),
        'b_fox_f': jax.random.uniform(ks[10], (DEPTH, H_FOX), f32, 1.0, 4.0),
        'g_pre': gain(ks[11], (DEPTH, D_MODEL)),
        'g_q': gain(ks[12], (DEPTH, HD_FOX)),
        'g_k': gain(ks[13], (DEPTH, HD_FOX)),
        'w_gla_a2': nrm(ks[14], (DEPTH, GLA_RANK, WK_GLA), GLA_RANK ** -0.5),
        'b_gla_a': nrm(ks[15], (DEPTH, WK_GLA), 0.1),
        'g_gla_o': gain(ks[16], (DEPTH, DV_GLA)),
        'b_merge': nrm(ks[17], (DEPTH, 2, D_MODEL), 0.1),
        'w_br_fox': nrm(ks[18], (DEPTH, W_FOX, D_MODEL), W_FOX ** -0.5),
        'w_br_gla': nrm(ks[19], (DEPTH, WV_GLA, D_MODEL), WV_GLA ** -0.5),
        'w_out': nrm(ks[20], (DEPTH, D_MODEL, D_MODEL), D_MODEL ** -0.5),
        'g_ple': gain(ks[21], (DEPTH, D_MODEL)),
        'w_ple_gate': nrm(ks[22], (DEPTH, D_MODEL, D_MODEL), D_MODEL ** -0.5),
        'b_ple_gate': nrm(ks[23], (DEPTH, D_MODEL), 0.1),
        'w_ple': nrm(ks[24], (DEPTH, PLE_DIM, D_MODEL), PLE_DIM ** -0.5),
    }


def reference(x_prompt, x_sample, p_prompt, p_sample, cache_k, cache_v, cache_logf, state_gla, page_table,
              w_in, b_fox_f, g_pre, g_q, g_k, w_gla_a2, b_gla_a, g_gla_o, b_merge, w_br_fox, w_br_gla,
              w_out, g_ple, w_ple_gate, b_ple_gate, w_ple):
    db, n_pages = page_table.shape
    xp, xs = x_prompt, x_sample
    kp_l, vp_l, lfp_l, sp_l = [], [], [], []
    ks_l, vs_l, lfs_l, ss_l = [], [], [], []
    for l in range(DEPTH):
        lw = {'g_pre': g_pre[l], 'w_in': w_in[l], 'b_fox_f': b_fox_f[l], 'g_q': g_q[l], 'g_k': g_k[l],
              'w_gla_a2': w_gla_a2[l], 'b_gla_a': b_gla_a[l], 'g_gla_o': g_gla_o[l], 'b_merge': b_merge[l],
              'w_br_fox': w_br_fox[l], 'w_br_gla': w_br_gla[l], 'w_out': w_out[l], 'g_ple': g_ple[l],
              'w_ple_gate': w_ple_gate[l], 'b_ple_gate': b_ple_gate[l], 'w_ple': w_ple[l]}
        xp, kp, vp, lfp, sp = mixer_layer(xp, p_prompt[l], lw, None)
        kp_l.append(kp); vp_l.append(vp); lfp_l.append(lfp); sp_l.append(sp)
        k_past = cache_k[l, page_table].reshape(db, n_pages * PAGE_SIZE, H_FOX, HD_FOX)
        v_past = cache_v[l, page_table].reshape(db, n_pages * PAGE_SIZE, H_FOX, HD_FOX)
        lf_past = cache_logf[l, page_table].reshape(db, n_pages * PAGE_SIZE, H_FOX)
        xs, kss, vss, lfs, ss = mixer_layer(xs, p_sample[l], lw, (k_past, v_past, lf_past, state_gla[l]))
        ks_l.append(kss); vs_l.append(vss); lfs_l.append(lfs); ss_l.append(ss)
    k_prompt = jnp.stack(kp_l)
    v_prompt = jnp.stack(vp_l)
    logf_prompt = jnp.stack(lfp_l)
    gla_prompt = jnp.stack(sp_l)
    k_sample = jnp.stack(ks_l)
    v_sample = jnp.stack(vs_l)
    logf_sample = jnp.stack(lfs_l)
    gla_sample = jnp.stack(ss_l)
    return (xp, xs, k_prompt, v_prompt, logf_prompt, gla_prompt, k_sample, v_sample, logf_sample, gla_sample)
```

```python
import functools

import jax
import jax.numpy as jnp
from jax import lax
from jax.experimental import pallas as pl
from jax.experimental.pallas import tpu as pltpu

F32 = jnp.float32
BF16 = jnp.bfloat16

D_MODEL = 1024
PLE_DIM = 256
HD_FOX = 64
W_FOX = 512
H_FOX = 8
H_GLA = 4
WK_GLA = 256
WV_GLA = 512
DK_GLA = 64
DV_GLA = 128
GLA_RANK = 16
GLA_CHUNK = 64
PAGE_SIZE = 128
EPS = 1e-6

_SIZES = (W_FOX, W_FOX, W_FOX, W_FOX, H_FOX, WK_GLA, WK_GLA, WV_GLA, WV_GLA, GLA_RANK, D_MODEL, D_MODEL)
_OFF = [0]
for _s in _SIZES:
    _OFF.append(_OFF[-1] + _s)
(_QF, _KF, _VF, _ZF, _FF, _QG, _KG, _VG, _ZG, _AG, _GF, _GG) = _OFF[:-1]

RANK_PAD = 128
_C_QF, _C_ZF, _C_QG, _C_KG, _C_VG, _C_ZG, _C_AG, _C_GF, _C_GG, N_STD = (
    0, 512, 1024, 1280, 1536, 2048, 2560, 2688, 3712, 4736)
N_TR = 1040

PAGES_PER_STEP = 8
SAMPLE_CHUNK = 16


def _log_sigmoid(x):
    return jnp.minimum(x, 0.0) - jnp.log1p(jnp.exp(-jnp.abs(x)))


def _silu(x):
    return x * jax.nn.sigmoid(x)


def _dot(a, b):
    return jnp.dot(a, b, preferred_element_type=F32)


def _dot_nt(a, b):
    return lax.dot_general(a, b, (((1,), (1,)), ((), ())), preferred_element_type=F32)


def _dot_tn(a, b):
    return lax.dot_general(a, b, (((0,), (0,)), ((), ())), preferred_element_type=F32)


def _cumsum_lanes(x):
    n = x.shape[-1]
    lane = lax.broadcasted_iota(jnp.int32, x.shape, x.ndim - 1)
    k = 1
    while k < n:
        x = x + jnp.where(lane >= k, pltpu.roll(x, k, axis=x.ndim - 1), 0.0)
        k *= 2
    return x


def _proj_kernel(x_ref, gpre_ref, wstd_ref, wtr_ref, bd_ref, gq_ref, gk_ref, wa2_ref, ba_ref, bff_ref,
                 bmf_ref, bmg_ref,
                 q_ref, szf_ref, qg_ref, kg_ref, vg_ref, szg_ref, la_ref, sgf_ref, sgg_ref,
                 kt_ref, vt_ref, ktb_ref, vtb_ref, lf_ref):
    x = x_ref[0]
    ta = x.shape[0]
    h = (x * lax.rsqrt(jnp.mean(x * x, axis=-1, keepdims=True) + EPS) * gpre_ref[0]).astype(BF16)

    def mm(lo, hi):
        return _dot(h, wstd_ref[0, :, lo:hi])

    qf = mm(_C_QF, _C_ZF)
    ss = _dot((qf * qf).astype(BF16), bd_ref[...]) * (1.0 / HD_FOX)
    q_ref[0] = (qf * lax.rsqrt(ss + EPS) * gq_ref[0]).astype(BF16)
    szf_ref[0] = _silu(mm(_C_ZF, _C_QG)).astype(BF16)
    qg_ref[0] = mm(_C_QG, _C_KG)
    kg_ref[0] = mm(_C_KG, _C_VG)
    vg_ref[0] = mm(_C_VG, _C_ZG).astype(vg_ref.dtype)
    szg_ref[0] = _silu(mm(_C_ZG, _C_AG)).astype(szg_ref.dtype)
    ag = mm(_C_AG, _C_GF)
    la_ref[0] = _log_sigmoid(_dot(ag.astype(BF16), wa2_ref[0]) + ba_ref[0]) * (1.0 / 16.0)
    sgf_ref[0] = jax.nn.sigmoid(mm(_C_GF, _C_GG) + bmf_ref[0]).astype(BF16)
    sgg_ref[0] = jax.nn.sigmoid(mm(_C_GG, N_STD) + bmg_ref[0]).astype(BF16)

    r = _dot_nt(wtr_ref[0], h)
    kt = r[0:W_FOX].reshape(H_FOX, HD_FOX, ta)
    kt = kt * lax.rsqrt(jnp.mean(kt * kt, axis=1, keepdims=True) + EPS) * gk_ref[0][None]
    kt = kt.reshape(W_FOX, ta)
    kt_ref[0] = kt
    ktb_ref[0] = kt.astype(BF16)
    vt = r[W_FOX:2 * W_FOX]
    vt_ref[0] = vt
    vtb_ref[0] = vt.astype(BF16)
    lf_ref[0] = _log_sigmoid(r[2 * W_FOX:2 * W_FOX + H_FOX] + bff_ref[0])


def _proj(x, l, wts, ta, gla_dtype):
    b_, l_, _ = x.shape
    grid = (b_, l_ // ta)
    tok = lambda w: pl.BlockSpec((1, ta, w), lambda b, t: (b, t, 0))
    feat = lambda r: pl.BlockSpec((1, r, ta), lambda b, t: (b, 0, t))
    lay = lambda *s: pl.BlockSpec((1,) + s, lambda b, t: (l,) + (0,) * len(s))
    const = lambda *s: pl.BlockSpec(s, lambda b, t: (0,) * len(s))
    sds = jax.ShapeDtypeStruct
    out_shape = (
        sds((b_, l_, W_FOX), BF16), sds((b_, l_, W_FOX), BF16),
        sds((b_, l_, WK_GLA), F32), sds((b_, l_, WK_GLA), F32),
        sds((b_, l_, WV_GLA), gla_dtype), sds((b_, l_, WV_GLA), gla_dtype),
        sds((b_, l_, WK_GLA), F32),
        sds((b_, l_, D_MODEL), BF16), sds((b_, l_, D_MODEL), BF16),
        sds((b_, W_FOX, l_), F32), sds((b_, W_FOX, l_), F32),
        sds((b_, W_FOX, l_), BF16), sds((b_, W_FOX, l_), BF16),
        sds((b_, H_FOX, l_), F32),
    )
    out_specs = (tok(W_FOX), tok(W_FOX), tok(WK_GLA), tok(WK_GLA), tok(WV_GLA), tok(WV_GLA), tok(WK_GLA),
                 tok(D_MODEL), tok(D_MODEL), feat(W_FOX), feat(W_FOX), feat(W_FOX), feat(W_FOX), feat(H_FOX))
    in_specs = [tok(D_MODEL), lay(1, D_MODEL), lay(D_MODEL, N_STD), lay(N_TR, D_MODEL), const(W_FOX, W_FOX),
                lay(1, W_FOX), lay(HD_FOX, 1), lay(RANK_PAD, WK_GLA), lay(1, WK_GLA), lay(H_FOX, 1),
                lay(1, D_MODEL), lay(1, D_MODEL)]
    return pl.pallas_call(
        _proj_kernel, grid=grid, in_specs=in_specs, out_specs=out_specs, out_shape=out_shape,
        compiler_params=pltpu.CompilerParams(dimension_semantics=("arbitrary", "arbitrary")),
        name="proj",
    )(x, wts["g_pre"], wts["w_std"], wts["w_tr"], wts["bd"], wts["g_q"], wts["g_k"], wts["w_a2"], wts["b_a"],
      wts["b_ff"], wts["b_mf"], wts["b_mg"])


def _fox_kernel(q_ref, kt_ref, vt_ref, lf_ref, o_ref, c_scr, m_scr, l_scr, acc_scr, *, tq):
    pr = pl.program_id(1)
    i = pl.program_id(2)

    @pl.when((pr == 0) & (i == 0))
    def _():
        c_scr[...] = _cumsum_lanes(lf_ref[0])

    q = q_ref[0]
    lane = lax.broadcasted_iota(jnp.int32, (1, 2 * HD_FOX), 1)
    qh = [jnp.where((lane // HD_FOX) == hh, q, jnp.zeros_like(q)) for hh in range(2)]
    for hh in range(2):
        m_scr[hh] = jnp.full((tq, 1), -jnp.inf, F32)
        l_scr[hh] = jnp.zeros((tq, 1), F32)
        acc_scr[hh] = jnp.zeros((tq, 2 * HD_FOX), F32)
    row = lax.broadcasted_iota(jnp.int32, (tq, tq), 0)
    col = lax.broadcasted_iota(jnp.int32, (tq, tq), 1)

    def step(j, masked):
        off = pl.multiple_of(j * tq, tq)
        kt = kt_ref[0, :, pl.ds(off, tq)]
        vt = vt_ref[0, :, pl.ds(off, tq)]
        for hh in range(2):
            s = _dot(qh[hh], kt) - c_scr[pl.ds(2 * pr + hh, 1), pl.ds(off, tq)]
            if masked:
                s = jnp.where(col <= row, s, -jnp.inf)
            m_prev = m_scr[hh]
            m_new = jnp.maximum(m_prev, jnp.max(s, axis=1, keepdims=True))
            alpha = jnp.exp(m_prev - m_new)
            p = jnp.exp(s - m_new)
            l_scr[hh] = alpha * l_scr[hh] + jnp.sum(p, axis=1, keepdims=True)
            acc_scr[hh] = alpha * acc_scr[hh] + _dot_nt(p.astype(BF16), vt)
            m_scr[hh] = m_new

    def body(j, carry):
        step(j, False)
        return carry

    lax.fori_loop(0, i, body, 0)
    step(i, True)
    o0 = acc_scr[0] / l_scr[0]
    o1 = acc_scr[1] / l_scr[1]
    o_ref[0] = jnp.where((lane // HD_FOX) == 0, o0, o1)


def _fox(q, ktb, vtb, lf, tq):
    b_, l_, _ = q.shape
    grid = (b_, H_FOX // 2, l_ // tq)
    return pl.pallas_call(
        functools.partial(_fox_kernel, tq=tq), grid=grid,
        in_specs=[pl.BlockSpec((1, tq, 2 * HD_FOX), lambda b, p, i: (b, i, p)),
                  pl.BlockSpec((1, 2 * HD_FOX, l_), lambda b, p, i: (b, p, 0)),
                  pl.BlockSpec((1, 2 * HD_FOX, l_), lambda b, p, i: (b, p, 0)),
                  pl.BlockSpec((1, H_FOX, l_), lambda b, p, i: (b, 0, 0))],
        out_specs=pl.BlockSpec((1, tq, 2 * HD_FOX), lambda b, p, i: (b, i, p)),
        out_shape=jax.ShapeDtypeStruct((b_, l_, W_FOX), F32),
        scratch_shapes=[pltpu.VMEM((H_FOX, l_), F32), pltpu.VMEM((2, tq, 1), F32), pltpu.VMEM((2, tq, 1), F32),
                        pltpu.VMEM((2, tq, 2 * HD_FOX), F32)],
        compiler_params=pltpu.CompilerParams(dimension_semantics=("arbitrary", "arbitrary", "arbitrary")),
        name="fox",
    )(q, ktb, vtb, lf)


def _decode_kernel(pt_ref, qbd_ref, *refs, n_steps):
    del pt_ref
    n = PAGES_PER_STEP
    k_refs, v_refs, lf_refs = refs[0:n], refs[n:2 * n], refs[2 * n:3 * n]
    knew_ref, vnew_ref, lfnew_ref, o_ref, m_scr, l_scr, acc_scr, coff_scr = refs[3 * n:]
    j = pl.program_id(1)
    rows = qbd_ref.shape[1]
    n_q = rows // H_FOX

    @pl.when(j == 0)
    def _():
        m_scr[...] = jnp.full(m_scr.shape, -jnp.inf, F32)
        l_scr[...] = jnp.zeros(l_scr.shape, F32)
        acc_scr[...] = jnp.zeros(acc_scr.shape, F32)
        coff_scr[...] = jnp.zeros(coff_scr.shape, F32)

    qbd = qbd_ref[0]

    def update(s, v_fn):
        m_prev = m_scr[...]
        m_new = jnp.maximum(m_prev, jnp.max(s, axis=1, keepdims=True))
        alpha = jnp.exp(m_prev - m_new)
        p = jnp.exp(s - m_new)
        l_scr[...] = alpha * l_scr[...] + jnp.sum(p, axis=1, keepdims=True)
        acc_scr[...] = alpha * acc_scr[...] + v_fn(p.astype(BF16))
        m_scr[...] = m_new

    kcat = jnp.concatenate([r[0, 0].reshape(W_FOX, PAGE_SIZE).astype(BF16) for r in k_refs], axis=1)
    vcat = jnp.concatenate([r[0, 0].reshape(W_FOX, PAGE_SIZE).astype(BF16) for r in v_refs], axis=1)
    lf = jnp.concatenate([r[0, 0] for r in lf_refs], axis=1)
    c = coff_scr[...] + _cumsum_lanes(lf)
    coff_scr[...] = c[:, n * PAGE_SIZE - 1:n * PAGE_SIZE]
    s = _dot(qbd, kcat) - jnp.concatenate([c] * n_q, axis=0)
    update(s, lambda p: _dot_nt(p, vcat))

    @pl.when(j == n_steps - 1)
    def _():
        pad = jnp.zeros((PAGE_SIZE - knew_ref.shape[1], W_FOX), F32)
        knew = jnp.concatenate([knew_ref[0], pad], axis=0).astype(BF16)
        vnew = jnp.concatenate([vnew_ref[0], pad], axis=0).astype(BF16)
        cn = coff_scr[...] + _cumsum_lanes(lfnew_ref[0])
        s2 = _dot_nt(qbd, knew) - jnp.concatenate([cn] * n_q, axis=0)
        qi = lax.broadcasted_iota(jnp.int32, s2.shape, 0) // H_FOX
        ki = lax.broadcasted_iota(jnp.int32, s2.shape, 1)
        s2 = jnp.where(ki <= qi, s2, -jnp.inf)
        update(s2, lambda p: _dot(p, vnew))
        o = acc_scr[...] / l_scr[...]
        hrow = lax.broadcasted_iota(jnp.int32, (H_FOX, W_FOX), 0)
        hcol = lax.broadcasted_iota(jnp.int32, (H_FOX, W_FOX), 1) // HD_FOX
        sel = jnp.where(hrow == hcol, 1.0, 0.0)
        o_ref[0] = jnp.sum(o.reshape(n_q, H_FOX, W_FOX) * sel[None], axis=1)


def _decode(l, page_table, qbd, cache_kt, cache_vt, cache_lft, knew, vnew, lfnew):
    db, n_pages = page_table.shape
    n = PAGES_PER_STEP
    n_steps = n_pages // n
    rows = qbd.shape[1]
    n_q = rows // H_FOX

    def page_spec(i, shape):
        return pl.BlockSpec((1, 1) + shape, lambda b, j, pt: (l, pt[b, j * n + i]) + (0,) * len(shape))

    in_specs = [pl.BlockSpec((1, rows, W_FOX), lambda b, j, pt: (b, 0, 0))]
    in_specs += [page_spec(i, (H_FOX, HD_FOX, PAGE_SIZE)) for i in range(n)]
    in_specs += [page_spec(i, (H_FOX, HD_FOX, PAGE_SIZE)) for i in range(n)]
    in_specs += [page_spec(i, (H_FOX, PAGE_SIZE)) for i in range(n)]
    in_specs += [pl.BlockSpec((1, knew.shape[1], W_FOX), lambda b, j, pt: (b, 0, 0)),
                 pl.BlockSpec((1, vnew.shape[1], W_FOX), lambda b, j, pt: (b, 0, 0)),
                 pl.BlockSpec((1, H_FOX, PAGE_SIZE), lambda b, j, pt: (b, 0, 0))]
    grid_spec = pltpu.PrefetchScalarGridSpec(
        num_scalar_prefetch=1, grid=(db, n_steps), in_specs=in_specs,
        out_specs=pl.BlockSpec((1, n_q, W_FOX), lambda b, j, pt: (b, 0, 0)),
        scratch_shapes=[pltpu.VMEM((rows, 1), F32), pltpu.VMEM((rows, 1), F32), pltpu.VMEM((rows, W_FOX), F32),
                        pltpu.VMEM((H_FOX, 1), F32)])
    return pl.pallas_call(
        functools.partial(_decode_kernel, n_steps=n_steps), grid_spec=grid_spec,
        out_shape=jax.ShapeDtypeStruct((db, n_q, W_FOX), F32),
        compiler_params=pltpu.CompilerParams(dimension_semantics=("arbitrary", "arbitrary")),
        name="decode",
    )(page_table, qbd, *([cache_kt] * n), *([cache_vt] * n), *([cache_lft] * n), knew, vnew, lfnew)


def _gla_kernel(q_ref, k_ref, la_ref, v_ref, szg_ref, s0_ref, gno_ref, o_ref, sfin_ref, st_scr, *, chunk, n_chunks):
    c_ = chunk
    row = lax.broadcasted_iota(jnp.int32, (c_, c_), 0)
    col = lax.broadcasted_iota(jnp.int32, (c_, c_), 1)
    tril = col <= row
    tril_b = jnp.where(tril, 1.0, 0.0).astype(BF16)
    lane = lax.broadcasted_iota(jnp.int32, (1, 2 * DK_GLA), 1)
    srow = lax.broadcasted_iota(jnp.int32, (2 * DV_GLA, 2 * DK_GLA), 0) // DV_GLA
    scol = lax.broadcasted_iota(jnp.int32, (2 * DV_GLA, 2 * DK_GLA), 1) // DK_GLA
    blockdiag = srow == scol
    zero = jnp.zeros((DK_GLA, DV_GLA), F32)

    for p in range(2):
        top = jnp.concatenate([s0_ref[0, 2 * p], zero], axis=1)
        bot = jnp.concatenate([zero, s0_ref[0, 2 * p + 1]], axis=1)
        st_scr[p] = jnp.concatenate([top, bot], axis=0).T

    def body(ci, carry):
        off = pl.multiple_of(ci * c_, c_)
        la = la_ref[0, pl.ds(off, c_), :]
        la_hi = la.astype(BF16)
        la_lo = (la - la_hi.astype(F32)).astype(BF16)
        bc = _dot(tril_b, la_hi) + _dot(tril_b, la_lo)
        bl = bc[c_ - 1:c_, :]
        q = q_ref[0, pl.ds(off, c_), :]
        k = k_ref[0, pl.ds(off, c_), :]
        qd = (q * jnp.exp(bc) * (DK_GLA ** -0.5)).astype(BF16)
        kin = (k * jnp.exp(-bc)).astype(BF16)
        kout = (k * jnp.exp(bl - bc)).astype(BF16)
        v = v_ref[0, pl.ds(off, c_), :].astype(BF16)
        dec = jnp.exp(bl)
        outs = []
        for p in range(2):
            ks = slice(2 * DK_GLA * p, 2 * DK_GLA * (p + 1))
            vs = slice(2 * DV_GLA * p, 2 * DV_GLA * (p + 1))
            qd_p, kin_p, kout_p, v_p = qd[:, ks], kin[:, ks], kout[:, ks], v[:, vs]
            st = st_scr[p]
            o_inter = _dot_nt(qd_p, st.astype(BF16))
            for hh in range(2):
                qh = jnp.where((lane // DK_GLA) == hh, qd_p, jnp.zeros_like(qd_p))
                a = jnp.where(tril, _dot_nt(qh, kin_p), 0.0)
                hs = slice(DV_GLA * hh, DV_GLA * (hh + 1))
                o_h = _dot(a.astype(BF16), v_p[:, hs]) + o_inter[:, hs]
                o_h = o_h * lax.rsqrt(jnp.mean(o_h * o_h, axis=-1, keepdims=True) + EPS) * gno_ref[0]
                outs.append(o_h)
            ds = _dot_tn(v_p, kout_p)
            st_scr[p] = st * dec[:, ks] + jnp.where(blockdiag, ds, 0.0)
        o = jnp.concatenate(outs, axis=1) * szg_ref[0, pl.ds(off, c_), :].astype(F32)
        o_ref[0, pl.ds(off, c_), :] = o.astype(o_ref.dtype)
        return carry

    lax.fori_loop(0, n_chunks, body, 0)
    for p in range(2):
        s = st_scr[p].T
        sfin_ref[0, 2 * p] = s[0:DK_GLA, 0:DV_GLA]
        sfin_ref[0, 2 * p + 1] = s[DK_GLA:2 * DK_GLA, DV_GLA:2 * DV_GLA]


def _gla(l, q, k, la, v, szg, s0, s0_layer, gno, chunk):
    b_, l_, _ = q.shape
    seq = lambda w: pl.BlockSpec((1, l_, w), lambda b: (b, 0, 0))
    if s0_layer is None:
        s0_spec = pl.BlockSpec((1, H_GLA, DK_GLA, DV_GLA), lambda b: (b, 0, 0, 0))
    else:
        s0_spec = pl.BlockSpec((None, 1, H_GLA, DK_GLA, DV_GLA), lambda b: (s0_layer, b, 0, 0, 0))
    return pl.pallas_call(
        functools.partial(_gla_kernel, chunk=chunk, n_chunks=l_ // chunk), grid=(b_,),
        in_specs=[seq(WK_GLA), seq(WK_GLA), seq(WK_GLA), seq(WV_GLA), seq(WV_GLA), s0_spec,
                  pl.BlockSpec((1, 1, DV_GLA), lambda b: (l, 0, 0))],
        out_specs=(seq(WV_GLA), pl.BlockSpec((1, H_GLA, DK_GLA, DV_GLA), lambda b: (b, 0, 0, 0))),
        out_shape=(jax.ShapeDtypeStruct((b_, l_, WV_GLA), BF16),
                   jax.ShapeDtypeStruct((b_, H_GLA, DK_GLA, DV_GLA), F32)),
        scratch_shapes=[pltpu.VMEM((2, 2 * DV_GLA, 2 * DK_GLA), F32)],
        compiler_params=pltpu.CompilerParams(dimension_semantics=("arbitrary",)),
        name="gla",
    )(q, k, la, v, szg, s0, gno)


def _out_kernel(x_ref, of_ref, szf_ref, ogz_ref, sgf_ref, sgg_ref, p_ref,
                wbf_ref, wbg_ref, wout_ref, wpg_ref, wple_ref, gple_ref, bpg_ref, y_ref):
    x = x_ref[...]
    yf = _dot((of_ref[...] * szf_ref[...].astype(F32)).astype(BF16), wbf_ref[0])
    yg = _dot(ogz_ref[...], wbg_ref[0])
    m = sgf_ref[...].astype(F32) * yf + sgg_ref[...].astype(F32) * yg
    x1 = x + _dot(m.astype(BF16), wout_ref[0])
    hp = (x1 * lax.rsqrt(jnp.mean(x1 * x1, axis=-1, keepdims=True) + EPS) * gple_ref[0]).astype(BF16)
    gate = jax.nn.sigmoid(_dot(hp, wpg_ref[0]) + bpg_ref[0])
    y_ref[...] = x1 + gate * _dot(p_ref[0].astype(BF16), wple_ref[0])


def _out(l, x, of, szf, ogz, sgf, sgg, p_all, wts, td):
    n_tok = x.shape[0]
    tok = lambda w: pl.BlockSpec((td, w), lambda t: (t, 0))
    lay = lambda *s: pl.BlockSpec((1,) + s, lambda t: (l,) + (0,) * len(s))
    return pl.pallas_call(
        _out_kernel, grid=(n_tok // td,),
        in_specs=[tok(D_MODEL), tok(W_FOX), tok(W_FOX), tok(WV_GLA), tok(D_MODEL), tok(D_MODEL),
                  pl.BlockSpec((1, td, PLE_DIM), lambda t: (l, t, 0)),
                  lay(W_FOX, D_MODEL), lay(WV_GLA, D_MODEL), lay(D_MODEL, D_MODEL), lay(D_MODEL, D_MODEL),
                  lay(PLE_DIM, D_MODEL), lay(1, D_MODEL), lay(1, D_MODEL)],
        out_specs=tok(D_MODEL),
        out_shape=jax.ShapeDtypeStruct((n_tok, D_MODEL), F32),
        compiler_params=pltpu.CompilerParams(dimension_semantics=("arbitrary",)),
        name="out",
    )(x, of, szf, ogz, sgf, sgg, p_all, wts["w_br_fox"], wts["w_br_gla"], wts["w_out"], wts["w_ple_gate"],
      wts["w_ple"], wts["g_ple"], wts["b_ple_gate"])


def _prepare_weights(w_in, b_fox_f, g_pre, g_q, g_k, w_gla_a2, b_gla_a, g_gla_o, b_merge, w_br_fox, w_br_gla,
                     w_out, g_ple, w_ple_gate, b_ple_gate, w_ple):
    depth = w_in.shape[0]
    cols = lambda a, n: w_in[:, :, a:a + n]
    a_pad = jnp.pad(cols(_AG, GLA_RANK), ((0, 0), (0, 0), (0, RANK_PAD - GLA_RANK)))
    w_std = jnp.concatenate([cols(_QF, W_FOX), cols(_ZF, W_FOX), cols(_QG, WK_GLA), cols(_KG, WK_GLA),
                             cols(_VG, WV_GLA), cols(_ZG, WV_GLA), a_pad, cols(_GF, D_MODEL), cols(_GG, D_MODEL)],
                            axis=-1).astype(BF16)
    w_tr = jnp.concatenate([cols(_KF, W_FOX), cols(_VF, W_FOX), cols(_FF, H_FOX)], axis=-1)
    w_tr = jnp.pad(jnp.swapaxes(w_tr, 1, 2), ((0, 0), (0, N_TR - 2 * W_FOX - H_FOX), (0, 0))).astype(BF16)
    head = jnp.arange(W_FOX) // HD_FOX
    return {
        "w_std": w_std, "w_tr": w_tr,
        "bd": (head[:, None] == head[None, :]).astype(BF16),
        "g_pre": g_pre[:, None, :],
        "g_q": jnp.tile(g_q, (1, H_FOX))[:, None, :] * (HD_FOX ** -0.5),
        "g_k": g_k[:, :, None],
        "w_a2": jnp.pad(w_gla_a2, ((0, 0), (0, RANK_PAD - GLA_RANK), (0, 0))).astype(BF16),
        "b_a": b_gla_a[:, None, :],
        "b_ff": b_fox_f[:, :, None],
        "b_mf": b_merge[:, 0:1, :], "b_mg": b_merge[:, 1:2, :],
        "g_gla_o": g_gla_o[:, None, :],
        "w_br_fox": w_br_fox.astype(BF16), "w_br_gla": w_br_gla.astype(BF16), "w_out": w_out.astype(BF16),
        "w_ple_gate": w_ple_gate.astype(BF16), "w_ple": w_ple.astype(BF16),
        "g_ple": g_ple[:, None, :], "b_ple_gate": b_ple_gate[:, None, :],
    }


def kernel(x_prompt, x_sample, p_prompt, p_sample, cache_k, cache_v, cache_logf, state_gla, page_table, w_in, b_fox_f, g_pre, g_q, g_k, w_gla_a2, b_gla_a, g_gla_o, b_merge, w_br_fox, w_br_gla, w_out, g_ple, w_ple_gate, b_ple_gate, w_ple):
    depth = w_in.shape[0]
    bp, lp, _ = x_prompt.shape
    db, ls, _ = x_sample.shape
    n_s = db * ls
    wts = _prepare_weights(w_in, b_fox_f, g_pre, g_q, g_k, w_gla_a2, b_gla_a, g_gla_o, b_merge, w_br_fox,
                           w_br_gla, w_out, g_ple, w_ple_gate, b_ple_gate, w_ple)
    cache_kt = jnp.transpose(cache_k, (0, 1, 3, 4, 2))
    cache_vt = jnp.transpose(cache_v, (0, 1, 3, 4, 2))
    cache_lft = jnp.transpose(cache_logf, (0, 1, 3, 2))
    pp = p_prompt.reshape(depth, bp * lp, PLE_DIM)
    ps = p_sample.reshape(depth, n_s, PLE_DIM)
    head_sel = (jnp.arange(H_FOX)[:, None] == (jnp.arange(W_FOX) // HD_FOX)[None, :]).astype(BF16)
    zero_state = jnp.zeros((bp, H_GLA, DK_GLA, DV_GLA), F32)
    spad = SAMPLE_CHUNK - ls

    xp = x_prompt
    xs = x_sample.reshape(1, n_s, D_MODEL)
    kp_l, vp_l, lfp_l, sp_l, ks_l, vs_l, lfs_l, ss_l = ([] for _ in range(8))
    for l in range(depth):
        (q, szf, qg, kg, vg, szg, la, sgf, sgg, kt, vt, ktb, vtb, lf) = _proj(xp, l, wts, 256, BF16)
        of = _fox(q, ktb, vtb, lf, 256)
        ogz, s_new = _gla(l, qg, kg, la, vg, szg, zero_state, None, wts["g_gla_o"], GLA_CHUNK)
        flat = lambda a: a.reshape(bp * lp, a.shape[-1])
        xp = _out(l, flat(xp), flat(of), flat(szf), flat(ogz), flat(sgf), flat(sgg), pp, wts, 512)
        xp = xp.reshape(bp, lp, D_MODEL)
        kp_l.append(kt); vp_l.append(vt); lfp_l.append(lf); sp_l.append(s_new)

        (q, szf, qg, kg, vg, szg, la, sgf, sgg, kt, vt, ktb, vtb, lf) = _proj(xs, l, wts, n_s, F32)
        k_new = kt[0].T.reshape(db, ls, W_FOX)
        v_new = vt[0].T.reshape(db, ls, W_FOX)
        lf_new = jnp.transpose(lf[0].reshape(H_FOX, db, ls), (1, 0, 2))
        qbd = (q.reshape(db, ls, 1, W_FOX) * head_sel[None, None]).reshape(db, ls * H_FOX, W_FOX)
        rpad = (-ls) % 8
        of = _decode(l, page_table, qbd, cache_kt, cache_vt, cache_lft,
                     jnp.pad(k_new, ((0, 0), (0, rpad), (0, 0))), jnp.pad(v_new, ((0, 0), (0, rpad), (0, 0))),
                     jnp.pad(lf_new, ((0, 0), (0, 0), (0, PAGE_SIZE - ls))))
        seqpad = lambda a: jnp.pad(a.reshape(db, ls, a.shape[-1]), ((0, 0), (0, spad), (0, 0)))
        ogz, s_new = _gla(l, seqpad(qg), seqpad(kg), seqpad(la), seqpad(vg), seqpad(szg), state_gla, l,
                          wts["g_gla_o"], SAMPLE_CHUNK)
        ogz = ogz[:, :ls].reshape(n_s, WV_GLA)
        xs = _out(l, xs[0], of.reshape(n_s, W_FOX), szf[0], ogz, sgf[0], sgg[0], ps, wts, n_s)
        xs = xs.reshape(1, n_s, D_MODEL)
        ks_l.append(k_new.reshape(db, ls, H_FOX, HD_FOX)); vs_l.append(v_new.reshape(db, ls, H_FOX, HD_FOX))
        lfs_l.append(jnp.transpose(lf_new, (0, 2, 1))); ss_l.append(s_new)

    def heads_out(ts):
        a = jnp.stack(ts).reshape(depth, bp, H_FOX, HD_FOX, lp)
        return jnp.transpose(a, (0, 1, 4, 2, 3))

    return (xp, xs.reshape(db, ls, D_MODEL), heads_out(kp_l), heads_out(vp_l),
            jnp.transpose(jnp.stack(lfp_l), (0, 1, 3, 2)), jnp.stack(sp_l),
            jnp.stack(ks_l), jnp.stack(vs_l), jnp.stack(lfs_l), jnp.stack(ss_l))
```

```python
import functools

import jax
import jax.numpy as jnp
from jax import lax
from jax.experimental import pallas as pl
from jax.experimental.pallas import tpu as pltpu

F32 = jnp.float32
BF16 = jnp.bfloat16

D_MODEL = 1024
PLE_DIM = 256
HD_FOX = 64
W_FOX = 512
H_FOX = 8
H_GLA = 4
WK_GLA = 256
WV_GLA = 512
DK_GLA = 64
DV_GLA = 128
GLA_RANK = 16
GLA_CHUNK = 64
PAGE_SIZE = 128
EPS = 1e-6
LANES = 128

_SIZES = (W_FOX, W_FOX, W_FOX, W_FOX, H_FOX, WK_GLA, WK_GLA, WV_GLA, WV_GLA, GLA_RANK, D_MODEL, D_MODEL)
_OFF = [0]
for _s in _SIZES:
    _OFF.append(_OFF[-1] + _s)
(_QF, _KF, _VF, _ZF, _FF, _QG, _KG, _VG, _ZG, _AG, _GF, _GG) = _OFF[:-1]

RANK_PAD = 128
_C_QF, _C_ZF, _C_QG, _C_KG, _C_VG, _C_ZG, _C_AG, _C_GF, _C_GG, N_STD = (
    0, 512, 1024, 1280, 1536, 2048, 2560, 2688, 3712, 4736)
N_TR = 1040

PAGES_PER_STEP = 8
SAMPLE_CHUNK = 16
PROJ_TILE = 256
FOX_TILE = 512
OUT_TILE = 512
GLA_GROUP = 8
GLA_TOKENS = 256


def _log_sigmoid(x):
    return jnp.minimum(x, 0.0) - jnp.log1p(jnp.exp(-jnp.abs(x)))


def _silu(x):
    return x * jax.nn.sigmoid(x)


def _dot(a, b):
    return jnp.dot(a, b, preferred_element_type=F32)


def _dot_nt(a, b):
    return lax.dot_general(a, b, (((1,), (1,)), ((), ())), preferred_element_type=F32)


def _dot_tn(a, b):
    return lax.dot_general(a, b, (((0,), (0,)), ((), ())), preferred_element_type=F32)


def _cumsum_lanes(x):
    n = x.shape[-1]
    lane = lax.broadcasted_iota(jnp.int32, x.shape, x.ndim - 1)
    k = 1
    while k < n:
        x = x + jnp.where(lane >= k, pltpu.roll(x, k, axis=x.ndim - 1), 0.0)
        k *= 2
    return x


def _proj_kernel(x_ref, gpre_ref, wstd_ref, wtr_ref, bd_ref, gq_ref, gk_ref, wa2_ref, ba_ref, bff_ref,
                 bmf_ref, bmg_ref, kt_all_ref, vt_all_ref,
                 q_ref, szf_ref, qg_ref, kg_ref, vg_ref, szg_ref, la_ref, sgf_ref, sgg_ref,
                 kt_ref, vt_ref, ktb_ref, vtb_ref, lf_ref):
    del kt_all_ref, vt_all_ref
    x = x_ref[0]
    ta = x.shape[0]
    h = (x * lax.rsqrt(jnp.mean(x * x, axis=-1, keepdims=True) + EPS) * gpre_ref[0]).astype(BF16)

    def mm(lo, hi):
        return _dot(h, wstd_ref[0, :, lo:hi])

    qf = mm(_C_QF, _C_ZF)
    ss = _dot((qf * qf).astype(BF16), bd_ref[...]) * (1.0 / HD_FOX)
    q_ref[0] = (qf * lax.rsqrt(ss + EPS) * gq_ref[0]).astype(BF16)
    szf_ref[0] = _silu(mm(_C_ZF, _C_QG)).astype(BF16)
    qg_ref[0] = mm(_C_QG, _C_KG)
    kg_ref[0] = mm(_C_KG, _C_VG)
    vg_ref[0] = mm(_C_VG, _C_ZG).astype(vg_ref.dtype)
    szg_ref[0] = _silu(mm(_C_ZG, _C_AG)).astype(szg_ref.dtype)
    ag = mm(_C_AG, _C_GF)
    la_ref[0] = _log_sigmoid(_dot(ag.astype(BF16), wa2_ref[0]) + ba_ref[0]) * (1.0 / 16.0)
    sgf_ref[0] = jax.nn.sigmoid(mm(_C_GF, _C_GG) + bmf_ref[0]).astype(BF16)
    sgg_ref[0] = jax.nn.sigmoid(mm(_C_GG, N_STD) + bmg_ref[0]).astype(BF16)

    r = _dot_nt(wtr_ref[0], h)
    kt = r[0:W_FOX].reshape(H_FOX, HD_FOX, ta)
    kt = kt * lax.rsqrt(jnp.mean(kt * kt, axis=1, keepdims=True) + EPS) * gk_ref[0][None]
    kt = kt.reshape(W_FOX, ta)
    kt_ref[0, 0] = kt
    ktb_ref[0] = kt.astype(BF16)
    vt = r[W_FOX:2 * W_FOX]
    vt_ref[0, 0] = vt
    vtb_ref[0] = vt.astype(BF16)
    lf_ref[0] = _log_sigmoid(r[2 * W_FOX:2 * W_FOX + H_FOX] + bff_ref[0])


def _proj(x, l, wts, ta, gla_dtype, kt_all, vt_all, l_out):
    b_, l_, _ = x.shape
    grid = (b_, l_ // ta)
    tok = lambda w: pl.BlockSpec((1, ta, w), lambda b, t: (b, t, 0))
    feat = lambda r: pl.BlockSpec((1, r, ta), lambda b, t: (b, 0, t))
    slot = pl.BlockSpec((1, 1, W_FOX, ta), lambda b, t: (l_out, b, 0, t))
    lay = lambda *s: pl.BlockSpec((1,) + s, lambda b, t: (l,) + (0,) * len(s))
    const = lambda *s: pl.BlockSpec(s, lambda b, t: (0,) * len(s))
    hbm = pl.BlockSpec(memory_space=pl.ANY)
    sds = jax.ShapeDtypeStruct
    out_shape = (
        sds((b_, l_, W_FOX), BF16), sds((b_, l_, W_FOX), BF16),
        sds((b_, l_, WK_GLA), F32), sds((b_, l_, WK_GLA), F32),
        sds((b_, l_, WV_GLA), gla_dtype), sds((b_, l_, WV_GLA), gla_dtype),
        sds((b_, l_, WK_GLA), F32),
        sds((b_, l_, D_MODEL), BF16), sds((b_, l_, D_MODEL), BF16),
        sds(kt_all.shape, F32), sds(vt_all.shape, F32),
        sds((b_, W_FOX, l_), BF16), sds((b_, W_FOX, l_), BF16),
        sds((b_, H_FOX, l_), F32),
    )
    out_specs = (tok(W_FOX), tok(W_FOX), tok(WK_GLA), tok(WK_GLA), tok(WV_GLA), tok(WV_GLA), tok(WK_GLA),
                 tok(D_MODEL), tok(D_MODEL), slot, slot, feat(W_FOX), feat(W_FOX), feat(H_FOX))
    in_specs = [tok(D_MODEL), lay(1, D_MODEL), lay(D_MODEL, N_STD), lay(N_TR, D_MODEL), const(W_FOX, W_FOX),
                lay(1, W_FOX), lay(HD_FOX, 1), lay(RANK_PAD, WK_GLA), lay(1, WK_GLA), lay(H_FOX, 1),
                lay(1, D_MODEL), lay(1, D_MODEL), hbm, hbm]
    return pl.pallas_call(
        _proj_kernel, grid=grid, in_specs=in_specs, out_specs=out_specs, out_shape=out_shape,
        input_output_aliases={12: 9, 13: 10},
        compiler_params=pltpu.CompilerParams(dimension_semantics=("arbitrary", "arbitrary")),
        name="proj",
    )(x, wts["g_pre"], wts["w_std"], wts["w_tr"], wts["bd"], wts["g_q"], wts["g_k"], wts["w_a2"], wts["b_a"],
      wts["b_ff"], wts["b_mf"], wts["b_mg"], kt_all, vt_all)


def _fox_kernel(q_ref, kt_ref, vt_ref, lf_ref, o_ref, c_scr, s_scr, p_scr, *, tq):
    pr = pl.program_id(1)
    i = pl.program_id(2)

    @pl.when((pr == 0) & (i == 0))
    def _():
        c_scr[...] = _cumsum_lanes(lf_ref[0])

    q = q_ref[0]
    lane = lax.broadcasted_iota(jnp.int32, (1, 2 * HD_FOX), 1)
    qh = [jnp.where((lane // HD_FOX) == hh, q, jnp.zeros_like(q)) for hh in range(2)]
    row = lax.broadcasted_iota(jnp.int32, (tq, tq), 0)
    col = lax.broadcasted_iota(jnp.int32, (tq, tq), 1)
    n_lane_tiles = tq // LANES

    def head(hh, n_blocks):
        crow = c_scr[pl.ds(2 * pr + hh, 1), :]
        mx = None
        for j in range(n_blocks):
            ks = slice(j * tq, (j + 1) * tq)
            s = _dot(qh[hh], kt_ref[0, :, ks]) - crow[:, ks]
            if j == n_blocks - 1:
                s = jnp.where(col <= row, s, -jnp.inf)
            s_scr[hh, :, ks] = s
            for c in range(n_lane_tiles):
                sc = s[:, c * LANES:(c + 1) * LANES]
                mx = sc if mx is None else jnp.maximum(mx, sc)
        m_b = jnp.broadcast_to(jnp.max(mx, axis=1, keepdims=True), (tq, LANES))
        lacc = None
        for c in range(n_blocks * n_lane_tiles):
            cs = slice(c * LANES, (c + 1) * LANES)
            pc = jnp.exp(s_scr[hh, :, cs] - m_b)
            lacc = pc if lacc is None else lacc + pc
            p_scr[hh, :, cs] = pc.astype(BF16)
        n_keys = n_blocks * tq
        acc = _dot_nt(p_scr[hh, :, 0:n_keys], vt_ref[0, :, 0:n_keys])
        return acc / jnp.sum(lacc, axis=1, keepdims=True)

    for k in range(kt_ref.shape[2] // tq):
        @pl.when(i == k)
        def _(k=k):
            o_ref[0] = jnp.where((lane // HD_FOX) == 0, head(0, k + 1), head(1, k + 1))


def _fox(q, ktb, vtb, lf, tq):
    b_, l_, _ = q.shape
    grid = (b_, H_FOX // 2, l_ // tq)
    return pl.pallas_call(
        functools.partial(_fox_kernel, tq=tq), grid=grid,
        in_specs=[pl.BlockSpec((1, tq, 2 * HD_FOX), lambda b, p, i: (b, i, p)),
                  pl.BlockSpec((1, 2 * HD_FOX, l_), lambda b, p, i: (b, p, 0)),
                  pl.BlockSpec((1, 2 * HD_FOX, l_), lambda b, p, i: (b, p, 0)),
                  pl.BlockSpec((1, H_FOX, l_), lambda b, p, i: (b, 0, 0))],
        out_specs=pl.BlockSpec((1, tq, 2 * HD_FOX), lambda b, p, i: (b, i, p)),
        out_shape=jax.ShapeDtypeStruct((b_, l_, W_FOX), F32),
        scratch_shapes=[pltpu.VMEM((H_FOX, l_), F32), pltpu.VMEM((2, tq, l_), F32), pltpu.VMEM((2, tq, l_), BF16)],
        compiler_params=pltpu.CompilerParams(dimension_semantics=("arbitrary", "arbitrary", "arbitrary")),
        name="fox",
    )(q, ktb, vtb, lf)


def _decode_kernel(pt_ref, qbd_ref, *refs, n_steps):
    del pt_ref
    n = PAGES_PER_STEP
    k_refs, v_refs, lf_refs = refs[0:n], refs[n:2 * n], refs[2 * n:3 * n]
    knew_ref, vnew_ref, lfnew_ref, o_ref, m_scr, l_scr, acc_scr, coff_scr = refs[3 * n:]
    j = pl.program_id(1)
    rows = qbd_ref.shape[1]
    n_q = rows // H_FOX

    @pl.when(j == 0)
    def _():
        m_scr[...] = jnp.full(m_scr.shape, -jnp.inf, F32)
        l_scr[...] = jnp.zeros(l_scr.shape, F32)
        acc_scr[...] = jnp.zeros(acc_scr.shape, F32)
        coff_scr[...] = jnp.zeros(coff_scr.shape, F32)

    qbd = qbd_ref[0]

    def update(s, v_fn):
        m_prev = m_scr[...]
        m_new = jnp.maximum(m_prev, jnp.max(s, axis=1, keepdims=True))
        alpha = jnp.exp(m_prev - m_new)
        p = jnp.exp(s - m_new)
        l_scr[...] = alpha * l_scr[...] + jnp.sum(p, axis=1, keepdims=True)
        acc_scr[...] = alpha * acc_scr[...] + v_fn(p.astype(BF16))
        m_scr[...] = m_new

    kcat = jnp.concatenate([r[0, 0].reshape(W_FOX, PAGE_SIZE).astype(BF16) for r in k_refs], axis=1)
    vcat = jnp.concatenate([r[0, 0].reshape(W_FOX, PAGE_SIZE).astype(BF16) for r in v_refs], axis=1)
    lf = jnp.concatenate([r[0, 0] for r in lf_refs], axis=1)
    c = coff_scr[...] + _cumsum_lanes(lf)
    coff_scr[...] = c[:, n * PAGE_SIZE - 1:n * PAGE_SIZE]
    s = _dot(qbd, kcat) - jnp.concatenate([c] * n_q, axis=0)
    update(s, lambda p: _dot_nt(p, vcat))

    @pl.when(j == n_steps - 1)
    def _():
        pad = jnp.zeros((PAGE_SIZE - knew_ref.shape[1], W_FOX), F32)
        knew = jnp.concatenate([knew_ref[0], pad], axis=0).astype(BF16)
        vnew = jnp.concatenate([vnew_ref[0], pad], axis=0).astype(BF16)
        cn = coff_scr[...] + _cumsum_lanes(lfnew_ref[0])
        s2 = _dot_nt(qbd, knew) - jnp.concatenate([cn] * n_q, axis=0)
        qi = lax.broadcasted_iota(jnp.int32, s2.shape, 0) // H_FOX
        ki = lax.broadcasted_iota(jnp.int32, s2.shape, 1)
        s2 = jnp.where(ki <= qi, s2, -jnp.inf)
        update(s2, lambda p: _dot(p, vnew))
        o = acc_scr[...] / l_scr[...]
        hrow = lax.broadcasted_iota(jnp.int32, (H_FOX, W_FOX), 0)
        hcol = lax.broadcasted_iota(jnp.int32, (H_FOX, W_FOX), 1) // HD_FOX
        sel = jnp.where(hrow == hcol, 1.0, 0.0)
        o_ref[0] = jnp.sum(o.reshape(n_q, H_FOX, W_FOX) * sel[None], axis=1)


def _decode(l, page_table, qbd, cache_kt, cache_vt, cache_lft, knew, vnew, lfnew):
    db, n_pages = page_table.shape
    n = PAGES_PER_STEP
    n_steps = n_pages // n
    rows = qbd.shape[1]
    n_q = rows // H_FOX

    def page_spec(i, shape):
        return pl.BlockSpec((1, 1) + shape, lambda b, j, pt: (l, pt[b, j * n + i]) + (0,) * len(shape))

    in_specs = [pl.BlockSpec((1, rows, W_FOX), lambda b, j, pt: (b, 0, 0))]
    in_specs += [page_spec(i, (H_FOX, HD_FOX, PAGE_SIZE)) for i in range(n)]
    in_specs += [page_spec(i, (H_FOX, HD_FOX, PAGE_SIZE)) for i in range(n)]
    in_specs += [page_spec(i, (H_FOX, PAGE_SIZE)) for i in range(n)]
    in_specs += [pl.BlockSpec((1, knew.shape[1], W_FOX), lambda b, j, pt: (b, 0, 0)),
                 pl.BlockSpec((1, vnew.shape[1], W_FOX), lambda b, j, pt: (b, 0, 0)),
                 pl.BlockSpec((1, H_FOX, PAGE_SIZE), lambda b, j, pt: (b, 0, 0))]
    grid_spec = pltpu.PrefetchScalarGridSpec(
        num_scalar_prefetch=1, grid=(db, n_steps), in_specs=in_specs,
        out_specs=pl.BlockSpec((1, n_q, W_FOX), lambda b, j, pt: (b, 0, 0)),
        scratch_shapes=[pltpu.VMEM((rows, 1), F32), pltpu.VMEM((rows, 1), F32), pltpu.VMEM((rows, W_FOX), F32),
                        pltpu.VMEM((H_FOX, 1), F32)])
    return pl.pallas_call(
        functools.partial(_decode_kernel, n_steps=n_steps), grid_spec=grid_spec,
        out_shape=jax.ShapeDtypeStruct((db, n_q, W_FOX), F32),
        compiler_params=pltpu.CompilerParams(dimension_semantics=("arbitrary", "arbitrary")),
        name="decode",
    )(page_table, qbd, *([cache_kt] * n), *([cache_vt] * n), *([cache_lft] * n), knew, vnew, lfnew)


def _gla_kernel(q_ref, k_ref, la_ref, v_ref, szg_ref, s0_ref, gno_ref, o_ref, sfin_ref, st_scr, *, chunk, n_chunks):
    c_ = chunk
    nb = q_ref.shape[0]
    t = pl.program_id(1)
    row = lax.broadcasted_iota(jnp.int32, (c_, c_), 0)
    col = lax.broadcasted_iota(jnp.int32, (c_, c_), 1)
    tril = col <= row
    tril_b = jnp.where(tril, 1.0, 0.0).astype(BF16)
    lane = lax.broadcasted_iota(jnp.int32, (1, 2 * DK_GLA), 1)
    srow = lax.broadcasted_iota(jnp.int32, (2 * DV_GLA, 2 * DK_GLA), 0) // DV_GLA
    scol = lax.broadcasted_iota(jnp.int32, (2 * DV_GLA, 2 * DK_GLA), 1) // DK_GLA
    blockdiag = srow == scol
    zero = jnp.zeros((DK_GLA, DV_GLA), F32)

    @pl.when(t == 0)
    def _():
        for bi in range(nb):
            for p in range(2):
                top = jnp.concatenate([s0_ref[bi, 2 * p], zero], axis=1)
                bot = jnp.concatenate([zero, s0_ref[bi, 2 * p + 1]], axis=1)
                st_scr[bi, p] = jnp.concatenate([top, bot], axis=0).T

    def advance(bi, off):
        la = la_ref[bi, pl.ds(off, c_), :]
        la_hi = la.astype(BF16)
        la_lo = (la - la_hi.astype(F32)).astype(BF16)
        bc = _dot(tril_b, la_hi) + _dot(tril_b, la_lo)
        bl = bc[c_ - 1:c_, :]
        q = q_ref[bi, pl.ds(off, c_), :]
        k = k_ref[bi, pl.ds(off, c_), :]
        qd = (q * jnp.exp(bc) * (DK_GLA ** -0.5)).astype(BF16)
        kin = (k * jnp.exp(-bc)).astype(BF16)
        kout = (k * jnp.exp(bl - bc)).astype(BF16)
        v = v_ref[bi, pl.ds(off, c_), :].astype(BF16)
        dec = jnp.exp(bl)
        outs = []
        for p in range(2):
            ks = slice(2 * DK_GLA * p, 2 * DK_GLA * (p + 1))
            vs = slice(2 * DV_GLA * p, 2 * DV_GLA * (p + 1))
            qd_p, kin_p, kout_p, v_p = qd[:, ks], kin[:, ks], kout[:, ks], v[:, vs]
            st = st_scr[bi, p]
            o_inter = _dot_nt(qd_p, st.astype(BF16))
            for hh in range(2):
                qh = jnp.where((lane // DK_GLA) == hh, qd_p, jnp.zeros_like(qd_p))
                a = jnp.where(tril, _dot_nt(qh, kin_p), 0.0)
                hs = slice(DV_GLA * hh, DV_GLA * (hh + 1))
                o_h = _dot(a.astype(BF16), v_p[:, hs]) + o_inter[:, hs]
                o_h = o_h * lax.rsqrt(jnp.mean(o_h * o_h, axis=-1, keepdims=True) + EPS) * gno_ref[0]
                outs.append(o_h)
            ds = _dot_tn(v_p, kout_p)
            st_scr[bi, p] = st * dec[:, ks] + jnp.where(blockdiag, ds, 0.0)
        o = jnp.concatenate(outs, axis=1) * szg_ref[bi, pl.ds(off, c_), :].astype(F32)
        o_ref[bi, pl.ds(off, c_), :] = o.astype(o_ref.dtype)

    def body(ci, carry):
        off = pl.multiple_of(ci * c_, c_)
        for bi in range(nb):
            advance(bi, off)
        return carry

    lax.fori_loop(0, n_chunks, body, 0)

    @pl.when(t == pl.num_programs(1) - 1)
    def _():
        for bi in range(nb):
            for p in range(2):
                s = st_scr[bi, p].T
                sfin_ref[bi, 2 * p] = s[0:DK_GLA, 0:DV_GLA]
                sfin_ref[bi, 2 * p + 1] = s[DK_GLA:2 * DK_GLA, DV_GLA:2 * DV_GLA]


def _gla(l, q, k, la, v, szg, s0, s0_layer, gno, chunk, nb, lt):
    b_, l_, _ = q.shape
    nb = min(nb, b_)
    seq = lambda w: pl.BlockSpec((nb, lt, w), lambda b, t: (b, t, 0))
    if s0_layer is None:
        s0_spec = pl.BlockSpec((nb, H_GLA, DK_GLA, DV_GLA), lambda b, t: (b, 0, 0, 0))
    else:
        s0_spec = pl.BlockSpec((None, nb, H_GLA, DK_GLA, DV_GLA), lambda b, t: (s0_layer, b, 0, 0, 0))
    return pl.pallas_call(
        functools.partial(_gla_kernel, chunk=chunk, n_chunks=lt // chunk), grid=(b_ // nb, l_ // lt),
        in_specs=[seq(WK_GLA), seq(WK_GLA), seq(WK_GLA), seq(WV_GLA), seq(WV_GLA), s0_spec,
                  pl.BlockSpec((1, 1, DV_GLA), lambda b, t: (l, 0, 0))],
        out_specs=(seq(WV_GLA), pl.BlockSpec((nb, H_GLA, DK_GLA, DV_GLA), lambda b, t: (b, 0, 0, 0))),
        out_shape=(jax.ShapeDtypeStruct((b_, l_, WV_GLA), BF16),
                   jax.ShapeDtypeStruct((b_, H_GLA, DK_GLA, DV_GLA), F32)),
        scratch_shapes=[pltpu.VMEM((nb, 2, 2 * DV_GLA, 2 * DK_GLA), F32)],
        compiler_params=pltpu.CompilerParams(dimension_semantics=("arbitrary", "arbitrary")),
        name="gla",
    )(q, k, la, v, szg, s0, gno)


def _out_kernel(x_ref, of_ref, szf_ref, ogz_ref, sgf_ref, sgg_ref, p_ref,
                wbf_ref, wbg_ref, wout_ref, wpg_ref, wple_ref, gple_ref, bpg_ref, y_ref):
    x = x_ref[...]
    yf = _dot((of_ref[...] * szf_ref[...].astype(F32)).astype(BF16), wbf_ref[0])
    yg = _dot(ogz_ref[...], wbg_ref[0])
    m = sgf_ref[...].astype(F32) * yf + sgg_ref[...].astype(F32) * yg
    x1 = x + _dot(m.astype(BF16), wout_ref[0])
    hp = (x1 * lax.rsqrt(jnp.mean(x1 * x1, axis=-1, keepdims=True) + EPS) * gple_ref[0]).astype(BF16)
    gate = jax.nn.sigmoid(_dot(hp, wpg_ref[0]) + bpg_ref[0])
    y_ref[...] = x1 + gate * _dot(p_ref[0].astype(BF16), wple_ref[0])


def _out(l, x, of, szf, ogz, sgf, sgg, p_all, wts, td):
    n_tok = x.shape[0]
    tok = lambda w: pl.BlockSpec((td, w), lambda t: (t, 0))
    lay = lambda *s: pl.BlockSpec((1,) + s, lambda t: (l,) + (0,) * len(s))
    return pl.pallas_call(
        _out_kernel, grid=(n_tok // td,),
        in_specs=[tok(D_MODEL), tok(W_FOX), tok(W_FOX), tok(WV_GLA), tok(D_MODEL), tok(D_MODEL),
                  pl.BlockSpec((1, td, PLE_DIM), lambda t: (l, t, 0)),
                  lay(W_FOX, D_MODEL), lay(WV_GLA, D_MODEL), lay(D_MODEL, D_MODEL), lay(D_MODEL, D_MODEL),
                  lay(PLE_DIM, D_MODEL), lay(1, D_MODEL), lay(1, D_MODEL)],
        out_specs=tok(D_MODEL),
        out_shape=jax.ShapeDtypeStruct((n_tok, D_MODEL), F32),
        compiler_params=pltpu.CompilerParams(dimension_semantics=("arbitrary",)),
        name="out",
    )(x, of, szf, ogz, sgf, sgg, p_all, wts["w_br_fox"], wts["w_br_gla"], wts["w_out"], wts["w_ple_gate"],
      wts["w_ple"], wts["g_ple"], wts["b_ple_gate"])


def _prepare_weights(w_in, b_fox_f, g_pre, g_q, g_k, w_gla_a2, b_gla_a, g_gla_o, b_merge, w_br_fox, w_br_gla,
                     w_out, g_ple, w_ple_gate, b_ple_gate, w_ple):
    depth = w_in.shape[0]
    cols = lambda a, n: w_in[:, :, a:a + n]
    a_pad = jnp.pad(cols(_AG, GLA_RANK), ((0, 0), (0, 0), (0, RANK_PAD - GLA_RANK)))
    w_std = jnp.concatenate([cols(_QF, W_FOX), cols(_ZF, W_FOX), cols(_QG, WK_GLA), cols(_KG, WK_GLA),
                             cols(_VG, WV_GLA), cols(_ZG, WV_GLA), a_pad, cols(_GF, D_MODEL), cols(_GG, D_MODEL)],
                            axis=-1).astype(BF16)
    w_tr = jnp.concatenate([cols(_KF, W_FOX), cols(_VF, W_FOX), cols(_FF, H_FOX)], axis=-1)
    w_tr = jnp.pad(jnp.swapaxes(w_tr, 1, 2), ((0, 0), (0, N_TR - 2 * W_FOX - H_FOX), (0, 0))).astype(BF16)
    head = jnp.arange(W_FOX) // HD_FOX
    return {
        "w_std": w_std, "w_tr": w_tr,
        "bd": (head[:, None] == head[None, :]).astype(BF16),
        "g_pre": g_pre[:, None, :],
        "g_q": jnp.tile(g_q, (1, H_FOX))[:, None, :] * (HD_FOX ** -0.5),
        "g_k": g_k[:, :, None],
        "w_a2": jnp.pad(w_gla_a2, ((0, 0), (0, RANK_PAD - GLA_RANK), (0, 0))).astype(BF16),
        "b_a": b_gla_a[:, None, :],
        "b_ff": b_fox_f[:, :, None],
        "b_mf": b_merge[:, 0:1, :], "b_mg": b_merge[:, 1:2, :],
        "g_gla_o": g_gla_o[:, None, :],
        "w_br_fox": w_br_fox.astype(BF16), "w_br_gla": w_br_gla.astype(BF16), "w_out": w_out.astype(BF16),
        "w_ple_gate": w_ple_gate.astype(BF16), "w_ple": w_ple.astype(BF16),
        "g_ple": g_ple[:, None, :], "b_ple_gate": b_ple_gate[:, None, :],
    }


def kernel(x_prompt, x_sample, p_prompt, p_sample, cache_k, cache_v, cache_logf, state_gla, page_table, w_in, b_fox_f, g_pre, g_q, g_k, w_gla_a2, b_gla_a, g_gla_o, b_merge, w_br_fox, w_br_gla, w_out, g_ple, w_ple_gate, b_ple_gate, w_ple):
    depth = w_in.shape[0]
    bp, lp, _ = x_prompt.shape
    db, ls, _ = x_sample.shape
    n_s = db * ls
    wts = _prepare_weights(w_in, b_fox_f, g_pre, g_q, g_k, w_gla_a2, b_gla_a, g_gla_o, b_merge, w_br_fox,
                           w_br_gla, w_out, g_ple, w_ple_gate, b_ple_gate, w_ple)
    cache_kt = jnp.transpose(cache_k, (0, 1, 3, 4, 2))
    cache_vt = jnp.transpose(cache_v, (0, 1, 3, 4, 2))
    cache_lft = jnp.transpose(cache_logf, (0, 1, 3, 2))
    pp = p_prompt.reshape(depth, bp * lp, PLE_DIM)
    ps = p_sample.reshape(depth, n_s, PLE_DIM)
    head_sel = (jnp.arange(H_FOX)[:, None] == (jnp.arange(W_FOX) // HD_FOX)[None, :]).astype(BF16)
    zero_state = jnp.zeros((bp, H_GLA, DK_GLA, DV_GLA), F32)
    spad = SAMPLE_CHUNK - ls

    xp = x_prompt
    xs = x_sample.reshape(1, n_s, D_MODEL)
    lfp_l, sp_l, ks_l, vs_l, lfs_l, ss_l = ([] for _ in range(6))
    kt_all = jnp.zeros((depth, bp, W_FOX, lp), F32)
    vt_all = jnp.zeros((depth, bp, W_FOX, lp), F32)
    kt_s = jnp.zeros((1, 1, W_FOX, n_s), F32)
    vt_s = jnp.zeros((1, 1, W_FOX, n_s), F32)
    for l in range(depth):
        (q, szf, qg, kg, vg, szg, la, sgf, sgg, kt_all, vt_all, ktb, vtb, lf) = _proj(
            xp, l, wts, PROJ_TILE, BF16, kt_all, vt_all, l)
        of = _fox(q, ktb, vtb, lf, FOX_TILE)
        ogz, s_new = _gla(l, qg, kg, la, vg, szg, zero_state, None, wts["g_gla_o"], GLA_CHUNK,
                          GLA_GROUP, GLA_TOKENS)
        flat = lambda a: a.reshape(bp * lp, a.shape[-1])
        xp = _out(l, flat(xp), flat(of), flat(szf), flat(ogz), flat(sgf), flat(sgg), pp, wts, OUT_TILE)
        xp = xp.reshape(bp, lp, D_MODEL)
        lfp_l.append(lf); sp_l.append(s_new)

        (q, szf, qg, kg, vg, szg, la, sgf, sgg, kt, vt, ktb, vtb, lf) = _proj(
            xs, l, wts, n_s, F32, kt_s, vt_s, 0)
        k_new = kt[0, 0].T.reshape(db, ls, W_FOX)
        v_new = vt[0, 0].T.reshape(db, ls, W_FOX)
        lf_new = jnp.transpose(lf[0].reshape(H_FOX, db, ls), (1, 0, 2))
        qbd = (q.reshape(db, ls, 1, W_FOX) * head_sel[None, None]).reshape(db, ls * H_FOX, W_FOX)
        rpad = (-ls) % 8
        of = _decode(l, page_table, qbd, cache_kt, cache_vt, cache_lft,
                     jnp.pad(k_new, ((0, 0), (0, rpad), (0, 0))), jnp.pad(v_new, ((0, 0), (0, rpad), (0, 0))),
                     jnp.pad(lf_new, ((0, 0), (0, 0), (0, PAGE_SIZE - ls))))
        seqpad = lambda a: jnp.pad(a.reshape(db, ls, a.shape[-1]), ((0, 0), (0, spad), (0, 0)))
        ogz, s_new = _gla(l, seqpad(qg), seqpad(kg), seqpad(la), seqpad(vg), seqpad(szg), state_gla, l,
                          wts["g_gla_o"], SAMPLE_CHUNK, GLA_GROUP, SAMPLE_CHUNK)
        ogz = ogz[:, :ls].reshape(n_s, WV_GLA)
        xs = _out(l, xs[0], of.reshape(n_s, W_FOX), szf[0], ogz, sgf[0], sgg[0], ps, wts, n_s)
        xs = xs.reshape(1, n_s, D_MODEL)
        ks_l.append(k_new.reshape(db, ls, H_FOX, HD_FOX)); vs_l.append(v_new.reshape(db, ls, H_FOX, HD_FOX))
        lfs_l.append(jnp.transpose(lf_new, (0, 2, 1))); ss_l.append(s_new)

    def heads_out(a):
        return jnp.transpose(a.reshape(depth, bp, H_FOX, HD_FOX, lp), (0, 1, 4, 2, 3))

    return (xp, xs.reshape(db, ls, D_MODEL), heads_out(kt_all), heads_out(vt_all),
            jnp.transpose(jnp.stack(lfp_l), (0, 1, 3, 2)), jnp.stack(sp_l),
            jnp.stack(ks_l), jnp.stack(vs_l), jnp.stack(lfs_l), jnp.stack(ss_l))
```

```python
import functools

import jax
import jax.numpy as jnp
from jax import lax
from jax.experimental import pallas as pl
from jax.experimental.pallas import tpu as pltpu

F32 = jnp.float32
BF16 = jnp.bfloat16

D_MODEL = 1024
PLE_DIM = 256
HD_FOX = 64
W_FOX = 512
H_FOX = 8
H_GLA = 4
WK_GLA = 256
WV_GLA = 512
DK_GLA = 64
DV_GLA = 128
GLA_RANK = 16
GLA_CHUNK = 64
PAGE_SIZE = 128
EPS = 1e-6
LANES = 128

_SIZES = (W_FOX, W_FOX, W_FOX, W_FOX, H_FOX, WK_GLA, WK_GLA, WV_GLA, WV_GLA, GLA_RANK, D_MODEL, D_MODEL)
_OFF = [0]
for _s in _SIZES:
    _OFF.append(_OFF[-1] + _s)
(_QF, _KF, _VF, _ZF, _FF, _QG, _KG, _VG, _ZG, _AG, _GF, _GG) = _OFF[:-1]

RANK_PAD = 128
_C_QF, _C_ZF, _C_QG, _C_KG, _C_VG, _C_ZG, _C_AG, _C_GF, _C_GG, N_STD = (
    0, 512, 1024, 1280, 1536, 2048, 2560, 2688, 3712, 4736)
N_TR = 1040

PAGES_PER_STEP = 16
SAMPLE_CHUNK = 16
PROJ_TILE = 256
FOX_TILE = 512
OUT_TILE = 512
GLA_GROUP = 8
GLA_TOKENS = 256


def _log_sigmoid(x):
    return jnp.minimum(x, 0.0) - jnp.log1p(jnp.exp(-jnp.abs(x)))


def _silu(x):
    return x * jax.nn.sigmoid(x)


def _dot(a, b):
    return jnp.dot(a, b, preferred_element_type=F32)


def _dot_nt(a, b):
    return lax.dot_general(a, b, (((1,), (1,)), ((), ())), preferred_element_type=F32)


def _dot_tn(a, b):
    return lax.dot_general(a, b, (((0,), (0,)), ((), ())), preferred_element_type=F32)


def _cumsum_lanes(x):
    n = x.shape[-1]
    lane = lax.broadcasted_iota(jnp.int32, x.shape, x.ndim - 1)
    k = 1
    while k < n:
        x = x + jnp.where(lane >= k, pltpu.roll(x, k, axis=x.ndim - 1), 0.0)
        k *= 2
    return x


def _proj_kernel(x_ref, gpre_ref, wstd_ref, wtr_ref, bd_ref, gq_ref, gk_ref, wa2_ref, ba_ref, bff_ref,
                 bmf_ref, bmg_ref, kt_all_ref, vt_all_ref,
                 q_ref, szf_ref, qg_ref, kg_ref, vg_ref, szg_ref, la_ref, sgf_ref, sgg_ref,
                 kt_ref, vt_ref, ktb_ref, vtb_ref, lf_ref):
    del kt_all_ref, vt_all_ref
    x = x_ref[0]
    ta = x.shape[0]
    h = (x * lax.rsqrt(jnp.mean(x * x, axis=-1, keepdims=True) + EPS) * gpre_ref[0]).astype(BF16)

    def mm(lo, hi):
        return _dot(h, wstd_ref[0, :, lo:hi])

    qf = mm(_C_QF, _C_ZF)
    ss = _dot((qf * qf).astype(BF16), bd_ref[...]) * (1.0 / HD_FOX)
    q_ref[0] = (qf * lax.rsqrt(ss + EPS) * gq_ref[0]).astype(BF16)
    szf_ref[0] = _silu(mm(_C_ZF, _C_QG)).astype(BF16)
    qg_ref[0] = mm(_C_QG, _C_KG)
    kg_ref[0] = mm(_C_KG, _C_VG)
    vg_ref[0] = mm(_C_VG, _C_ZG).astype(vg_ref.dtype)
    szg_ref[0] = _silu(mm(_C_ZG, _C_AG)).astype(szg_ref.dtype)
    ag = mm(_C_AG, _C_GF)
    la_ref[0] = _log_sigmoid(_dot(ag.astype(BF16), wa2_ref[0]) + ba_ref[0]) * (1.0 / 16.0)
    sgf_ref[0] = jax.nn.sigmoid(mm(_C_GF, _C_GG) + bmf_ref[0]).astype(BF16)
    sgg_ref[0] = jax.nn.sigmoid(mm(_C_GG, N_STD) + bmg_ref[0]).astype(BF16)

    r = _dot_nt(wtr_ref[0], h)
    kt = r[0:W_FOX].reshape(H_FOX, HD_FOX, ta)
    kt = kt * lax.rsqrt(jnp.mean(kt * kt, axis=1, keepdims=True) + EPS) * gk_ref[0][None]
    kt = kt.reshape(W_FOX, ta)
    kt_ref[0, 0] = kt
    ktb_ref[0] = kt.astype(BF16)
    vt = r[W_FOX:2 * W_FOX]
    vt_ref[0, 0] = vt
    vtb_ref[0] = vt.astype(BF16)
    lf_ref[0] = _log_sigmoid(r[2 * W_FOX:2 * W_FOX + H_FOX] + bff_ref[0])


def _proj(x, l, wts, ta, gla_dtype, kt_all, vt_all, l_out):
    b_, l_, _ = x.shape
    grid = (b_, l_ // ta)
    tok = lambda w: pl.BlockSpec((1, ta, w), lambda b, t: (b, t, 0))
    feat = lambda r: pl.BlockSpec((1, r, ta), lambda b, t: (b, 0, t))
    slot = pl.BlockSpec((1, 1, W_FOX, ta), lambda b, t: (l_out, b, 0, t))
    lay = lambda *s: pl.BlockSpec((1,) + s, lambda b, t: (l,) + (0,) * len(s))
    const = lambda *s: pl.BlockSpec(s, lambda b, t: (0,) * len(s))
    hbm = pl.BlockSpec(memory_space=pl.ANY)
    sds = jax.ShapeDtypeStruct
    out_shape = (
        sds((b_, l_, W_FOX), BF16), sds((b_, l_, W_FOX), BF16),
        sds((b_, l_, WK_GLA), F32), sds((b_, l_, WK_GLA), F32),
        sds((b_, l_, WV_GLA), gla_dtype), sds((b_, l_, WV_GLA), gla_dtype),
        sds((b_, l_, WK_GLA), F32),
        sds((b_, l_, D_MODEL), BF16), sds((b_, l_, D_MODEL), BF16),
        sds(kt_all.shape, F32), sds(vt_all.shape, F32),
        sds((b_, W_FOX, l_), BF16), sds((b_, W_FOX, l_), BF16),
        sds((b_, H_FOX, l_), F32),
    )
    out_specs = (tok(W_FOX), tok(W_FOX), tok(WK_GLA), tok(WK_GLA), tok(WV_GLA), tok(WV_GLA), tok(WK_GLA),
                 tok(D_MODEL), tok(D_MODEL), slot, slot, feat(W_FOX), feat(W_FOX), feat(H_FOX))
    in_specs = [tok(D_MODEL), lay(1, D_MODEL), lay(D_MODEL, N_STD), lay(N_TR, D_MODEL), const(W_FOX, W_FOX),
                lay(1, W_FOX), lay(HD_FOX, 1), lay(RANK_PAD, WK_GLA), lay(1, WK_GLA), lay(H_FOX, 1),
                lay(1, D_MODEL), lay(1, D_MODEL), hbm, hbm]
    return pl.pallas_call(
        _proj_kernel, grid=grid, in_specs=in_specs, out_specs=out_specs, out_shape=out_shape,
        input_output_aliases={12: 9, 13: 10},
        compiler_params=pltpu.CompilerParams(dimension_semantics=("arbitrary", "arbitrary")),
        name="proj",
    )(x, wts["g_pre"], wts["w_std"], wts["w_tr"], wts["bd"], wts["g_q"], wts["g_k"], wts["w_a2"], wts["b_a"],
      wts["b_ff"], wts["b_mf"], wts["b_mg"], kt_all, vt_all)


def _fox_kernel(q_ref, kt_ref, vt_ref, lf_ref, o_ref, c_scr, s_scr, p_scr, *, tq):
    pr = pl.program_id(1)
    i = pl.program_id(2)

    @pl.when((pr == 0) & (i == 0))
    def _():
        c_scr[...] = _cumsum_lanes(lf_ref[0])

    q = q_ref[0]
    lane = lax.broadcasted_iota(jnp.int32, (1, 2 * HD_FOX), 1)
    qh = [jnp.where((lane // HD_FOX) == hh, q, jnp.zeros_like(q)) for hh in range(2)]
    row = lax.broadcasted_iota(jnp.int32, (tq, tq), 0)
    col = lax.broadcasted_iota(jnp.int32, (tq, tq), 1)
    n_lane_tiles = tq // LANES

    def variant(n_blocks):
        crow = [c_scr[pl.ds(2 * pr + hh, 1), :] for hh in range(2)]
        mx, m_b, lacc, acc = [None, None], [None, None], [None, None], [None, None]

        def logits(hh, j):
            ks = slice(j * tq, (j + 1) * tq)
            s = _dot(qh[hh], kt_ref[0, :, ks]) - crow[hh][:, ks]
            if j == n_blocks - 1:
                s = jnp.where(col <= row, s, -jnp.inf)
            s_scr[hh, :, ks] = s
            for c in range(n_lane_tiles):
                sc = s[:, c * LANES:(c + 1) * LANES]
                mx[hh] = sc if mx[hh] is None else jnp.maximum(mx[hh], sc)

        def row_max(hh):
            m_b[hh] = jnp.broadcast_to(jnp.max(mx[hh], axis=1, keepdims=True), (tq, LANES))

        def exps(hh, j):
            for c in range(j * n_lane_tiles, (j + 1) * n_lane_tiles):
                cs = slice(c * LANES, (c + 1) * LANES)
                pc = jnp.exp(s_scr[hh, :, cs] - m_b[hh])
                lacc[hh] = pc if lacc[hh] is None else lacc[hh] + pc
                p_scr[hh, :, cs] = pc.astype(BF16)

        def values(hh, j):
            ks = slice(j * tq, (j + 1) * tq)
            pv = _dot_nt(p_scr[hh, :, ks], vt_ref[0, :, ks])
            acc[hh] = pv if acc[hh] is None else acc[hh] + pv

        blocks = range(n_blocks)
        for j in blocks:
            logits(0, j)
        row_max(0)
        for j in blocks:
            logits(1, j)
            exps(0, j)
        row_max(1)
        for j in blocks:
            exps(1, j)
            values(0, j)
        for j in blocks:
            values(1, j)
        o = [acc[hh] / jnp.sum(lacc[hh], axis=1, keepdims=True) for hh in range(2)]
        o_ref[0] = jnp.where((lane // HD_FOX) == 0, o[0], o[1])

    for k in range(kt_ref.shape[2] // tq):
        @pl.when(i == k)
        def _(k=k):
            variant(k + 1)


def _fox(q, ktb, vtb, lf, tq):
    b_, l_, _ = q.shape
    grid = (b_, H_FOX // 2, l_ // tq)
    return pl.pallas_call(
        functools.partial(_fox_kernel, tq=tq), grid=grid,
        in_specs=[pl.BlockSpec((1, tq, 2 * HD_FOX), lambda b, p, i: (b, i, p)),
                  pl.BlockSpec((1, 2 * HD_FOX, l_), lambda b, p, i: (b, p, 0)),
                  pl.BlockSpec((1, 2 * HD_FOX, l_), lambda b, p, i: (b, p, 0)),
                  pl.BlockSpec((1, H_FOX, l_), lambda b, p, i: (b, 0, 0))],
        out_specs=pl.BlockSpec((1, tq, 2 * HD_FOX), lambda b, p, i: (b, i, p)),
        out_shape=jax.ShapeDtypeStruct((b_, l_, W_FOX), F32),
        scratch_shapes=[pltpu.VMEM((H_FOX, l_), F32), pltpu.VMEM((2, tq, l_), F32), pltpu.VMEM((2, tq, l_), BF16)],
        compiler_params=pltpu.CompilerParams(dimension_semantics=("arbitrary", "arbitrary", "arbitrary")),
        name="fox",
    )(q, ktb, vtb, lf)


def _decode_kernel(pt_ref, qbd_ref, qbd_prev_ref, *refs, n_steps, n_groups):
    del pt_ref
    n = PAGES_PER_STEP
    k_refs, v_refs, lf_refs = refs[0:n], refs[n:2 * n], refs[2 * n:3 * n]
    knew_ref, vnew_ref, lfnew_ref, o_ref, m_scr, l_scr, acc_scr, coff_scr, ctot_scr, s_scr = refs[3 * n:]
    t = pl.program_id(0)
    rows = qbd_ref.shape[1]
    n_q = rows // H_FOX
    slot_w = t % 2
    slot_r = 1 - slot_w
    first_of_seq = (t % n_steps) == 0

    @pl.when(t == 0)
    def _():
        coff_scr[...] = jnp.zeros(coff_scr.shape, F32)
        ctot_scr[...] = jnp.zeros(ctot_scr.shape, F32)
        s_scr[1] = jnp.full(s_scr.shape[1:], -jnp.inf, F32)

    @pl.when((t == 0) | ((t - 1) % n_steps == 0))
    def _():
        m_scr[...] = jnp.full(m_scr.shape, jnp.finfo(F32).min, F32)
        l_scr[...] = jnp.zeros(l_scr.shape, F32)
        acc_scr[...] = jnp.zeros(acc_scr.shape, F32)

    qbd = qbd_ref[0]

    def update(s, v_fn):
        m_prev = m_scr[...]
        m_new = jnp.maximum(m_prev, jnp.max(s, axis=1, keepdims=True))
        alpha = jnp.exp(m_prev - m_new)
        p = jnp.exp(s - m_new)
        l_scr[...] = alpha * l_scr[...] + jnp.sum(p, axis=1, keepdims=True)
        acc_scr[...] = alpha * acc_scr[...] + v_fn(p.astype(BF16))
        m_scr[...] = m_new

    vcat = jnp.concatenate([r[0, 0].reshape(W_FOX, PAGE_SIZE).astype(BF16) for r in v_refs], axis=1)
    update(s_scr[slot_r], lambda p: _dot_nt(p, vcat))

    kcat = jnp.concatenate([r[0, 0].reshape(W_FOX, PAGE_SIZE).astype(BF16) for r in k_refs], axis=1)
    lf = jnp.concatenate([r[0, 0] for r in lf_refs], axis=1)
    c = jnp.where(first_of_seq, 0.0, coff_scr[...]) + _cumsum_lanes(lf)
    c_end = c[:, n * PAGE_SIZE - 1:n * PAGE_SIZE]
    live = t < n_groups
    coff_scr[...] = jnp.where(live, c_end, coff_scr[...])
    ctot_scr[...] = jnp.where(live & ((t % n_steps) == n_steps - 1), c_end, ctot_scr[...])
    s_scr[slot_w] = _dot(qbd, kcat) - jnp.concatenate([c] * n_q, axis=0)

    @pl.when((t > 0) & first_of_seq)
    def _():
        qbd_prev = qbd_prev_ref[0]
        pad = jnp.zeros((PAGE_SIZE - knew_ref.shape[1], W_FOX), F32)
        knew = jnp.concatenate([knew_ref[0], pad], axis=0).astype(BF16)
        vnew = jnp.concatenate([vnew_ref[0], pad], axis=0).astype(BF16)
        cn = ctot_scr[...] + _cumsum_lanes(lfnew_ref[0])
        s2 = _dot_nt(qbd_prev, knew) - jnp.concatenate([cn] * n_q, axis=0)
        qi = lax.broadcasted_iota(jnp.int32, s2.shape, 0) // H_FOX
        ki = lax.broadcasted_iota(jnp.int32, s2.shape, 1)
        s2 = jnp.where(ki <= qi, s2, -jnp.inf)
        update(s2, lambda p: _dot(p, vnew))
        o = acc_scr[...] / l_scr[...]
        hrow = lax.broadcasted_iota(jnp.int32, (H_FOX, W_FOX), 0)
        hcol = lax.broadcasted_iota(jnp.int32, (H_FOX, W_FOX), 1) // HD_FOX
        sel = jnp.where(hrow == hcol, 1.0, 0.0)
        o_ref[0] = jnp.sum(o.reshape(n_q, H_FOX, W_FOX) * sel[None], axis=1)


def _decode(l, page_table, qbd, cache_kt, cache_vt, cache_lft, knew, vnew, lfnew):
    db, n_pages = page_table.shape
    n = PAGES_PER_STEP
    n_steps = n_pages // n
    rows = qbd.shape[1]
    n_q = rows // H_FOX

    assert n_steps >= 2
    n_groups = db * n_steps

    def key_group(t):
        return jnp.minimum(t, n_groups - 1)

    def val_group(t):
        return jnp.maximum(t - 1, 0)

    def page_spec(i, shape, group):
        def index(t, pt):
            g = group(t)
            return (l, pt[g // n_steps, (g % n_steps) * n + i]) + (0,) * len(shape)
        return pl.BlockSpec((1, 1) + shape, index)

    def seq_spec(shape, group):
        return pl.BlockSpec((1,) + shape, lambda t, pt: (group(t) // n_steps,) + (0,) * len(shape))

    in_specs = [seq_spec((rows, W_FOX), key_group), seq_spec((rows, W_FOX), val_group)]
    in_specs += [page_spec(i, (H_FOX, HD_FOX, PAGE_SIZE), key_group) for i in range(n)]
    in_specs += [page_spec(i, (H_FOX, HD_FOX, PAGE_SIZE), val_group) for i in range(n)]
    in_specs += [page_spec(i, (H_FOX, PAGE_SIZE), key_group) for i in range(n)]
    in_specs += [seq_spec((knew.shape[1], W_FOX), val_group), seq_spec((vnew.shape[1], W_FOX), val_group),
                 seq_spec((H_FOX, PAGE_SIZE), val_group)]
    grid_spec = pltpu.PrefetchScalarGridSpec(
        num_scalar_prefetch=1, grid=(n_groups + 1,), in_specs=in_specs,
        out_specs=seq_spec((n_q, W_FOX), val_group),
        scratch_shapes=[pltpu.VMEM((rows, 1), F32), pltpu.VMEM((rows, 1), F32), pltpu.VMEM((rows, W_FOX), F32),
                        pltpu.VMEM((H_FOX, 1), F32), pltpu.VMEM((H_FOX, 1), F32),
                        pltpu.VMEM((2, rows, n * PAGE_SIZE), F32)])
    return pl.pallas_call(
        functools.partial(_decode_kernel, n_steps=n_steps, n_groups=n_groups), grid_spec=grid_spec,
        out_shape=jax.ShapeDtypeStruct((db, n_q, W_FOX), F32),
        compiler_params=pltpu.CompilerParams(dimension_semantics=("arbitrary",)),
        name="decode",
    )(page_table, qbd, qbd, *([cache_kt] * n), *([cache_vt] * n), *([cache_lft] * n), knew, vnew, lfnew)


def _gla_kernel(q_ref, k_ref, la_ref, v_ref, szg_ref, s0_ref, gno_ref, o_ref, sfin_ref, st_scr, *, chunk, n_chunks):
    c_ = chunk
    nb = q_ref.shape[0]
    t = pl.program_id(1)
    row = lax.broadcasted_iota(jnp.int32, (c_, c_), 0)
    col = lax.broadcasted_iota(jnp.int32, (c_, c_), 1)
    tril = col <= row
    tril_b = jnp.where(tril, 1.0, 0.0).astype(BF16)
    lane = lax.broadcasted_iota(jnp.int32, (1, 2 * DK_GLA), 1)
    srow = lax.broadcasted_iota(jnp.int32, (2 * DV_GLA, 2 * DK_GLA), 0) // DV_GLA
    scol = lax.broadcasted_iota(jnp.int32, (2 * DV_GLA, 2 * DK_GLA), 1) // DK_GLA
    blockdiag = srow == scol
    zero = jnp.zeros((DK_GLA, DV_GLA), F32)

    @pl.when(t == 0)
    def _():
        for bi in range(nb):
            for p in range(2):
                top = jnp.concatenate([s0_ref[bi, 2 * p], zero], axis=1)
                bot = jnp.concatenate([zero, s0_ref[bi, 2 * p + 1]], axis=1)
                st_scr[bi, p] = jnp.concatenate([top, bot], axis=0).T

    def advance(bi, off):
        la = la_ref[bi, pl.ds(off, c_), :]
        la_hi = la.astype(BF16)
        la_lo = (la - la_hi.astype(F32)).astype(BF16)
        bc = _dot(tril_b, la_hi) + _dot(tril_b, la_lo)
        bl = bc[c_ - 1:c_, :]
        q = q_ref[bi, pl.ds(off, c_), :]
        k = k_ref[bi, pl.ds(off, c_), :]
        qd = (q * jnp.exp(bc) * (DK_GLA ** -0.5)).astype(BF16)
        kin = (k * jnp.exp(-bc)).astype(BF16)
        kout = (k * jnp.exp(bl - bc)).astype(BF16)
        v = v_ref[bi, pl.ds(off, c_), :].astype(BF16)
        dec = jnp.exp(bl)
        outs = []
        for p in range(2):
            ks = slice(2 * DK_GLA * p, 2 * DK_GLA * (p + 1))
            vs = slice(2 * DV_GLA * p, 2 * DV_GLA * (p + 1))
            qd_p, kin_p, kout_p, v_p = qd[:, ks], kin[:, ks], kout[:, ks], v[:, vs]
            st = st_scr[bi, p]
            o_inter = _dot_nt(qd_p, st.astype(BF16))
            for hh in range(2):
                qh = jnp.where((lane // DK_GLA) == hh, qd_p, jnp.zeros_like(qd_p))
                a = jnp.where(tril, _dot_nt(qh, kin_p), 0.0)
                hs = slice(DV_GLA * hh, DV_GLA * (hh + 1))
                o_h = _dot(a.astype(BF16), v_p[:, hs]) + o_inter[:, hs]
                o_h = o_h * lax.rsqrt(jnp.mean(o_h * o_h, axis=-1, keepdims=True) + EPS) * gno_ref[0]
                outs.append(o_h)
            ds = _dot_tn(v_p, kout_p)
            st_scr[bi, p] = st * dec[:, ks] + jnp.where(blockdiag, ds, 0.0)
        o = jnp.concatenate(outs, axis=1) * szg_ref[bi, pl.ds(off, c_), :].astype(F32)
        o_ref[bi, pl.ds(off, c_), :] = o.astype(o_ref.dtype)

    def body(ci, carry):
        off = pl.multiple_of(ci * c_, c_)
        for bi in range(nb):
            advance(bi, off)
        return carry

    lax.fori_loop(0, n_chunks, body, 0)

    @pl.when(t == pl.num_programs(1) - 1)
    def _():
        for bi in range(nb):
            for p in range(2):
                s = st_scr[bi, p].T
                sfin_ref[bi, 2 * p] = s[0:DK_GLA, 0:DV_GLA]
                sfin_ref[bi, 2 * p + 1] = s[DK_GLA:2 * DK_GLA, DV_GLA:2 * DV_GLA]


def _gla(l, q, k, la, v, szg, s0, s0_layer, gno, chunk, nb, lt):
    b_, l_, _ = q.shape
    nb = min(nb, b_)
    seq = lambda w: pl.BlockSpec((nb, lt, w), lambda b, t: (b, t, 0))
    if s0_layer is None:
        s0_spec = pl.BlockSpec((nb, H_GLA, DK_GLA, DV_GLA), lambda b, t: (b, 0, 0, 0))
    else:
        s0_spec = pl.BlockSpec((None, nb, H_GLA, DK_GLA, DV_GLA), lambda b, t: (s0_layer, b, 0, 0, 0))
    return pl.pallas_call(
        functools.partial(_gla_kernel, chunk=chunk, n_chunks=lt // chunk), grid=(b_ // nb, l_ // lt),
        in_specs=[seq(WK_GLA), seq(WK_GLA), seq(WK_GLA), seq(WV_GLA), seq(WV_GLA), s0_spec,
                  pl.BlockSpec((1, 1, DV_GLA), lambda b, t: (l, 0, 0))],
        out_specs=(seq(WV_GLA), pl.BlockSpec((nb, H_GLA, DK_GLA, DV_GLA), lambda b, t: (b, 0, 0, 0))),
        out_shape=(jax.ShapeDtypeStruct((b_, l_, WV_GLA), BF16),
                   jax.ShapeDtypeStruct((b_, H_GLA, DK_GLA, DV_GLA), F32)),
        scratch_shapes=[pltpu.VMEM((nb, 2, 2 * DV_GLA, 2 * DK_GLA), F32)],
        compiler_params=pltpu.CompilerParams(dimension_semantics=("arbitrary", "arbitrary")),
        name="gla",
    )(q, k, la, v, szg, s0, gno)


def _out_kernel(x_ref, of_ref, szf_ref, ogz_ref, sgf_ref, sgg_ref, p_ref,
                wbf_ref, wbg_ref, wout_ref, wpg_ref, wple_ref, gple_ref, bpg_ref, y_ref):
    x = x_ref[...]
    yf = _dot((of_ref[...] * szf_ref[...].astype(F32)).astype(BF16), wbf_ref[0])
    yg = _dot(ogz_ref[...], wbg_ref[0])
    m = sgf_ref[...].astype(F32) * yf + sgg_ref[...].astype(F32) * yg
    x1 = x + _dot(m.astype(BF16), wout_ref[0])
    hp = (x1 * lax.rsqrt(jnp.mean(x1 * x1, axis=-1, keepdims=True) + EPS) * gple_ref[0]).astype(BF16)
    gate = jax.nn.sigmoid(_dot(hp, wpg_ref[0]) + bpg_ref[0])
    y_ref[...] = x1 + gate * _dot(p_ref[0].astype(BF16), wple_ref[0])


def _out(l, x, of, szf, ogz, sgf, sgg, p_all, wts, td):
    n_tok = x.shape[0]
    tok = lambda w: pl.BlockSpec((td, w), lambda t: (t, 0))
    lay = lambda *s: pl.BlockSpec((1,) + s, lambda t: (l,) + (0,) * len(s))
    return pl.pallas_call(
        _out_kernel, grid=(n_tok // td,),
        in_specs=[tok(D_MODEL), tok(W_FOX), tok(W_FOX), tok(WV_GLA), tok(D_MODEL), tok(D_MODEL),
                  pl.BlockSpec((1, td, PLE_DIM), lambda t: (l, t, 0)),
                  lay(W_FOX, D_MODEL), lay(WV_GLA, D_MODEL), lay(D_MODEL, D_MODEL), lay(D_MODEL, D_MODEL),
                  lay(PLE_DIM, D_MODEL), lay(1, D_MODEL), lay(1, D_MODEL)],
        out_specs=tok(D_MODEL),
        out_shape=jax.ShapeDtypeStruct((n_tok, D_MODEL), F32),
        compiler_params=pltpu.CompilerParams(dimension_semantics=("arbitrary",)),
        name="out",
    )(x, of, szf, ogz, sgf, sgg, p_all, wts["w_br_fox"], wts["w_br_gla"], wts["w_out"], wts["w_ple_gate"],
      wts["w_ple"], wts["g_ple"], wts["b_ple_gate"])


def _prepare_weights(w_in, b_fox_f, g_pre, g_q, g_k, w_gla_a2, b_gla_a, g_gla_o, b_merge, w_br_fox, w_br_gla,
                     w_out, g_ple, w_ple_gate, b_ple_gate, w_ple):
    depth = w_in.shape[0]
    cols = lambda a, n: w_in[:, :, a:a + n]
    a_pad = jnp.pad(cols(_AG, GLA_RANK), ((0, 0), (0, 0), (0, RANK_PAD - GLA_RANK)))
    w_std = jnp.concatenate([cols(_QF, W_FOX), cols(_ZF, W_FOX), cols(_QG, WK_GLA), cols(_KG, WK_GLA),
                             cols(_VG, WV_GLA), cols(_ZG, WV_GLA), a_pad, cols(_GF, D_MODEL), cols(_GG, D_MODEL)],
                            axis=-1).astype(BF16)
    w_tr = jnp.concatenate([cols(_KF, W_FOX), cols(_VF, W_FOX), cols(_FF, H_FOX)], axis=-1)
    w_tr = jnp.pad(jnp.swapaxes(w_tr, 1, 2), ((0, 0), (0, N_TR - 2 * W_FOX - H_FOX), (0, 0))).astype(BF16)
    head = jnp.arange(W_FOX) // HD_FOX
    return {
        "w_std": w_std, "w_tr": w_tr,
        "bd": (head[:, None] == head[None, :]).astype(BF16),
        "g_pre": g_pre[:, None, :],
        "g_q": jnp.tile(g_q, (1, H_FOX))[:, None, :] * (HD_FOX ** -0.5),
        "g_k": g_k[:, :, None],
        "w_a2": jnp.pad(w_gla_a2, ((0, 0), (0, RANK_PAD - GLA_RANK), (0, 0))).astype(BF16),
        "b_a": b_gla_a[:, None, :],
        "b_ff": b_fox_f[:, :, None],
        "b_mf": b_merge[:, 0:1, :], "b_mg": b_merge[:, 1:2, :],
        "g_gla_o": g_gla_o[:, None, :],
        "w_br_fox": w_br_fox.astype(BF16), "w_br_gla": w_br_gla.astype(BF16), "w_out": w_out.astype(BF16),
        "w_ple_gate": w_ple_gate.astype(BF16), "w_ple": w_ple.astype(BF16),
        "g_ple": g_ple[:, None, :], "b_ple_gate": b_ple_gate[:, None, :],
    }


def kernel(x_prompt, x_sample, p_prompt, p_sample, cache_k, cache_v, cache_logf, state_gla, page_table, w_in, b_fox_f, g_pre, g_q, g_k, w_gla_a2, b_gla_a, g_gla_o, b_merge, w_br_fox, w_br_gla, w_out, g_ple, w_ple_gate, b_ple_gate, w_ple):
    depth = w_in.shape[0]
    bp, lp, _ = x_prompt.shape
    db, ls, _ = x_sample.shape
    n_s = db * ls
    wts = _prepare_weights(w_in, b_fox_f, g_pre, g_q, g_k, w_gla_a2, b_gla_a, g_gla_o, b_merge, w_br_fox,
                           w_br_gla, w_out, g_ple, w_ple_gate, b_ple_gate, w_ple)
    cache_kt = jnp.transpose(cache_k, (0, 1, 3, 4, 2))
    cache_vt = jnp.transpose(cache_v, (0, 1, 3, 4, 2))
    cache_lft = jnp.transpose(cache_logf, (0, 1, 3, 2))
    pp = p_prompt.reshape(depth, bp * lp, PLE_DIM)
    ps = p_sample.reshape(depth, n_s, PLE_DIM)
    head_sel = (jnp.arange(H_FOX)[:, None] == (jnp.arange(W_FOX) // HD_FOX)[None, :]).astype(BF16)
    zero_state = jnp.zeros((bp, H_GLA, DK_GLA, DV_GLA), F32)
    spad = SAMPLE_CHUNK - ls

    xp = x_prompt
    xs = x_sample.reshape(1, n_s, D_MODEL)
    lfp_l, sp_l, ks_l, vs_l, lfs_l, ss_l = ([] for _ in range(6))
    kt_all = jnp.zeros((depth, bp, W_FOX, lp), F32)
    vt_all = jnp.zeros((depth, bp, W_FOX, lp), F32)
    kt_s = jnp.zeros((1, 1, W_FOX, n_s), F32)
    vt_s = jnp.zeros((1, 1, W_FOX, n_s), F32)
    for l in range(depth):
        (q, szf, qg, kg, vg, szg, la, sgf, sgg, kt_all, vt_all, ktb, vtb, lf) = _proj(
            xp, l, wts, PROJ_TILE, BF16, kt_all, vt_all, l)
        of = _fox(q, ktb, vtb, lf, FOX_TILE)
        ogz, s_new = _gla(l, qg, kg, la, vg, szg, zero_state, None, wts["g_gla_o"], GLA_CHUNK,
                          GLA_GROUP, GLA_TOKENS)
        flat = lambda a: a.reshape(bp * lp, a.shape[-1])
        xp = _out(l, flat(xp), flat(of), flat(szf), flat(ogz), flat(sgf), flat(sgg), pp, wts, OUT_TILE)
        xp = xp.reshape(bp, lp, D_MODEL)
        lfp_l.append(lf); sp_l.append(s_new)

        (q, szf, qg, kg, vg, szg, la, sgf, sgg, kt, vt, ktb, vtb, lf) = _proj(
            xs, l, wts, n_s, F32, kt_s, vt_s, 0)
        k_new = kt[0, 0].T.reshape(db, ls, W_FOX)
        v_new = vt[0, 0].T.reshape(db, ls, W_FOX)
        lf_new = jnp.transpose(lf[0].reshape(H_FOX, db, ls), (1, 0, 2))
        qbd = (q.reshape(db, ls, 1, W_FOX) * head_sel[None, None]).reshape(db, ls * H_FOX, W_FOX)
        rpad = (-ls) % 8
        of = _decode(l, page_table, qbd, cache_kt, cache_vt, cache_lft,
                     jnp.pad(k_new, ((0, 0), (0, rpad), (0, 0))), jnp.pad(v_new, ((0, 0), (0, rpad), (0, 0))),
                     jnp.pad(lf_new, ((0, 0), (0, 0), (0, PAGE_SIZE - ls))))
        seqpad = lambda a: jnp.pad(a.reshape(db, ls, a.shape[-1]), ((0, 0), (0, spad), (0, 0)))
        ogz, s_new = _gla(l, seqpad(qg), seqpad(kg), seqpad(la), seqpad(vg), seqpad(szg), state_gla, l,
                          wts["g_gla_o"], SAMPLE_CHUNK, GLA_GROUP, SAMPLE_CHUNK)
        ogz = ogz[:, :ls].reshape(n_s, WV_GLA)
        xs = _out(l, xs[0], of.reshape(n_s, W_FOX), szf[0], ogz, sgf[0], sgg[0], ps, wts, n_s)
        xs = xs.reshape(1, n_s, D_MODEL)
        ks_l.append(k_new.reshape(db, ls, H_FOX, HD_FOX)); vs_l.append(v_new.reshape(db, ls, H_FOX, HD_FOX))
        lfs_l.append(jnp.transpose(lf_new, (0, 2, 1))); ss_l.append(s_new)

    def heads_out(a):
        return jnp.transpose(a.reshape(depth, bp, H_FOX, HD_FOX, lp), (0, 1, 4, 2, 3))

    return (xp, xs.reshape(db, ls, D_MODEL), heads_out(kt_all), heads_out(vt_all),
            jnp.transpose(jnp.stack(lfp_l), (0, 1, 3, 2)), jnp.stack(sp_l),
            jnp.stack(ks_l), jnp.stack(vs_l), jnp.stack(lfs_l), jnp.stack(ss_l))
```

```python
import functools

import jax
import jax.numpy as jnp
from jax import lax
from jax.experimental import pallas as pl
from jax.experimental.pallas import tpu as pltpu

F32 = jnp.float32
BF16 = jnp.bfloat16

D_MODEL = 1024
PLE_DIM = 256
HD_FOX = 64
W_FOX = 512
H_FOX = 8
H_GLA = 4
WK_GLA = 256
WV_GLA = 512
DK_GLA = 64
DV_GLA = 128
GLA_RANK = 16
GLA_CHUNK = 64
PAGE_SIZE = 128
EPS = 1e-6
LANES = 128

_SIZES = (W_FOX, W_FOX, W_FOX, W_FOX, H_FOX, WK_GLA, WK_GLA, WV_GLA, WV_GLA, GLA_RANK, D_MODEL, D_MODEL)
_OFF = [0]
for _s in _SIZES:
    _OFF.append(_OFF[-1] + _s)
(_QF, _KF, _VF, _ZF, _FF, _QG, _KG, _VG, _ZG, _AG, _GF, _GG) = _OFF[:-1]

RANK_PAD = 128
_C_QF, _C_ZF, _C_QG, _C_KG, _C_VG, _C_ZG, _C_AG, _C_GF, _C_GG, N_STD = (
    0, 512, 1024, 1280, 1536, 2048, 2560, 2688, 3712, 4736)
N_TR = 1040

PAGES_PER_STEP = 16
DECODE_PIECE = 4
SAMPLE_CHUNK = 16
PROJ_TILE = 256
FOX_TILE = 512
FOX_ROWS = 256
FOX_HEADS = 4
OUT_TILE = 512
GLA_GROUP = 8
GLA_TOKENS = 256


def _log_sigmoid(x):
    return jnp.minimum(x, 0.0) - jnp.log1p(jnp.exp(-jnp.abs(x)))


def _silu(x):
    return x * jax.nn.sigmoid(x)


def _dot(a, b):
    return jnp.dot(a, b, preferred_element_type=F32)


def _dot_nt(a, b):
    return lax.dot_general(a, b, (((1,), (1,)), ((), ())), preferred_element_type=F32)


def _dot_tn(a, b):
    return lax.dot_general(a, b, (((0,), (0,)), ((), ())), preferred_element_type=F32)


def _cumsum_lanes(x):
    n = x.shape[-1]
    lane = lax.broadcasted_iota(jnp.int32, x.shape, x.ndim - 1)
    k = 1
    while k < n:
        x = x + jnp.where(lane >= k, pltpu.roll(x, k, axis=x.ndim - 1), 0.0)
        k *= 2
    return x


def _proj_kernel(x_ref, gpre_ref, wstd_ref, wtr_ref, bd_ref, gq_ref, gk_ref, wa2_ref, ba_ref, bff_ref,
                 bmf_ref, bmg_ref, kt_all_ref, vt_all_ref,
                 q_ref, szf_ref, qg_ref, kg_ref, vg_ref, szg_ref, la_ref, sgf_ref, sgg_ref,
                 kt_ref, vt_ref, ktb_ref, vtb_ref, lf_ref):
    del kt_all_ref, vt_all_ref
    x = x_ref[0]
    ta = x.shape[0]
    h = (x * lax.rsqrt(jnp.mean(x * x, axis=-1, keepdims=True) + EPS) * gpre_ref[0]).astype(BF16)

    def mm(lo, hi):
        return _dot(h, wstd_ref[0, :, lo:hi])

    qf = mm(_C_QF, _C_ZF)
    ss = _dot((qf * qf).astype(BF16), bd_ref[...]) * (1.0 / HD_FOX)
    q_ref[0] = (qf * lax.rsqrt(ss + EPS) * gq_ref[0]).astype(BF16)
    szf_ref[0] = _silu(mm(_C_ZF, _C_QG)).astype(BF16)
    qg_ref[0] = mm(_C_QG, _C_KG)
    kg_ref[0] = mm(_C_KG, _C_VG)
    vg_ref[0] = mm(_C_VG, _C_ZG).astype(vg_ref.dtype)
    szg_ref[0] = _silu(mm(_C_ZG, _C_AG)).astype(szg_ref.dtype)
    ag = mm(_C_AG, _C_GF)
    la_ref[0] = _log_sigmoid(_dot(ag.astype(BF16), wa2_ref[0]) + ba_ref[0]) * (1.0 / 16.0)
    sgf_ref[0] = jax.nn.sigmoid(mm(_C_GF, _C_GG) + bmf_ref[0]).astype(BF16)
    sgg_ref[0] = jax.nn.sigmoid(mm(_C_GG, N_STD) + bmg_ref[0]).astype(BF16)

    r = _dot_nt(wtr_ref[0], h)
    kt = r[0:W_FOX].reshape(H_FOX, HD_FOX, ta)
    kt = kt * lax.rsqrt(jnp.mean(kt * kt, axis=1, keepdims=True) + EPS) * gk_ref[0][None]
    kt = kt.reshape(W_FOX, ta)
    kt_ref[0, 0] = kt
    ktb_ref[0] = kt.astype(BF16)
    vt = r[W_FOX:2 * W_FOX]
    vt_ref[0, 0] = vt
    vtb_ref[0] = vt.astype(BF16)
    lf_ref[0] = _log_sigmoid(r[2 * W_FOX:2 * W_FOX + H_FOX] + bff_ref[0])


def _proj(x, l, wts, ta, gla_dtype, kt_all, vt_all, l_out):
    b_, l_, _ = x.shape
    grid = (b_, l_ // ta)
    tok = lambda w: pl.BlockSpec((1, ta, w), lambda b, t: (b, t, 0))
    feat = lambda r: pl.BlockSpec((1, r, ta), lambda b, t: (b, 0, t))
    slot = pl.BlockSpec((1, 1, W_FOX, ta), lambda b, t: (l_out, b, 0, t))
    lay = lambda *s: pl.BlockSpec((1,) + s, lambda b, t: (l,) + (0,) * len(s))
    const = lambda *s: pl.BlockSpec(s, lambda b, t: (0,) * len(s))
    hbm = pl.BlockSpec(memory_space=pl.ANY)
    sds = jax.ShapeDtypeStruct
    out_shape = (
        sds((b_, l_, W_FOX), BF16), sds((b_, l_, W_FOX), BF16),
        sds((b_, l_, WK_GLA), F32), sds((b_, l_, WK_GLA), F32),
        sds((b_, l_, WV_GLA), gla_dtype), sds((b_, l_, WV_GLA), gla_dtype),
        sds((b_, l_, WK_GLA), F32),
        sds((b_, l_, D_MODEL), BF16), sds((b_, l_, D_MODEL), BF16),
        sds(kt_all.shape, F32), sds(vt_all.shape, F32),
        sds((b_, W_FOX, l_), BF16), sds((b_, W_FOX, l_), BF16),
        sds((b_, H_FOX, l_), F32),
    )
    out_specs = (tok(W_FOX), tok(W_FOX), tok(WK_GLA), tok(WK_GLA), tok(WV_GLA), tok(WV_GLA), tok(WK_GLA),
                 tok(D_MODEL), tok(D_MODEL), slot, slot, feat(W_FOX), feat(W_FOX), feat(H_FOX))
    in_specs = [tok(D_MODEL), lay(1, D_MODEL), lay(D_MODEL, N_STD), lay(N_TR, D_MODEL), const(W_FOX, W_FOX),
                lay(1, W_FOX), lay(HD_FOX, 1), lay(RANK_PAD, WK_GLA), lay(1, WK_GLA), lay(H_FOX, 1),
                lay(1, D_MODEL), lay(1, D_MODEL), hbm, hbm]
    return pl.pallas_call(
        _proj_kernel, grid=grid, in_specs=in_specs, out_specs=out_specs, out_shape=out_shape,
        input_output_aliases={12: 9, 13: 10},
        compiler_params=pltpu.CompilerParams(dimension_semantics=("arbitrary", "arbitrary")),
        name="proj",
    )(x, wts["g_pre"], wts["w_std"], wts["w_tr"], wts["bd"], wts["g_q"], wts["g_k"], wts["w_a2"], wts["b_a"],
      wts["b_ff"], wts["b_mf"], wts["b_mg"], kt_all, vt_all)


def _fox_kernel(q_ref, kt_ref, vt_ref, lf_ref, o_ref, c_scr, s_scr, p_scr, *, tq):
    pr = pl.program_id(1)
    i = pl.program_id(2)
    n_heads = q_ref.shape[2] // HD_FOX

    @pl.when((pr == 0) & (i == 0))
    def _():
        c_scr[...] = _cumsum_lanes(lf_ref[0])

    lane = lax.broadcasted_iota(jnp.int32, (1, 2 * HD_FOX), 1)
    pair_rows = [slice(2 * HD_FOX * (h // 2), 2 * HD_FOX * (h // 2 + 1)) for h in range(n_heads)]
    qh = []
    for h in range(n_heads):
        q_pair = q_ref[0, :, pair_rows[h]]
        qh.append(jnp.where((lane // HD_FOX) == h % 2, q_pair, jnp.zeros_like(q_pair)))
    rg = FOX_ROWS
    n_rg = tq // rg
    col = lax.broadcasted_iota(jnp.int32, (rg, tq), 1)
    rows = [lax.broadcasted_iota(jnp.int32, (rg, tq), 0) + r * rg for r in range(n_rg)]
    n_lane_tiles = tq // LANES

    def variant(n_blocks):
        crow = [c_scr[pl.ds(n_heads * pr + hh, 1), :] for hh in range(n_heads)]
        mx, m_b, lacc, acc = {}, {}, {}, {}

        def logits(hh, j, r):
            ks = slice(j * tq, (j + 1) * tq)
            rs = slice(r * rg, (r + 1) * rg)
            s = _dot(qh[hh][rs], kt_ref[0, pair_rows[hh], ks]) - crow[hh][:, ks]
            if j == n_blocks - 1:
                s = jnp.where(col <= rows[r], s, -jnp.inf)
            s_scr[hh, rs, ks] = s
            for c in range(n_lane_tiles):
                sc = s[:, c * LANES:(c + 1) * LANES]
                mx[hh, r] = sc if (hh, r) not in mx else jnp.maximum(mx[hh, r], sc)

        def row_max(hh):
            for r in range(n_rg):
                m_b[hh, r] = jnp.broadcast_to(jnp.max(mx[hh, r], axis=1, keepdims=True), (rg, LANES))

        def exps(hh, j, r):
            rs = slice(r * rg, (r + 1) * rg)
            for c in range(j * n_lane_tiles, (j + 1) * n_lane_tiles):
                cs = slice(c * LANES, (c + 1) * LANES)
                pc = jnp.exp(s_scr[hh, rs, cs] - m_b[hh, r])
                lacc[hh, r] = pc if (hh, r) not in lacc else lacc[hh, r] + pc
                p_scr[hh, rs, cs] = pc.astype(BF16)

        def values(hh, j, r):
            ks = slice(j * tq, (j + 1) * tq)
            rs = slice(r * rg, (r + 1) * rg)
            pv = _dot_nt(p_scr[hh, rs, ks], vt_ref[0, pair_rows[hh], ks])
            acc[hh, r] = pv if (hh, r) not in acc else acc[hh, r] + pv

        items = [(j, r) for j in range(n_blocks) for r in range(n_rg)]
        for step in range(n_heads + 2):
            for j, r in items:
                if step < n_heads:
                    logits(step, j, r)
                if 0 <= step - 1 < n_heads:
                    exps(step - 1, j, r)
                if 0 <= step - 2 < n_heads:
                    values(step - 2, j, r)
            if step < n_heads:
                row_max(step)
        o = [jnp.concatenate([acc[hh, r] / jnp.sum(lacc[hh, r], axis=1, keepdims=True) for r in range(n_rg)],
                             axis=0) for hh in range(n_heads)]
        o_ref[0] = jnp.concatenate([jnp.where((lane // HD_FOX) == 0, o[h], o[h + 1])
                                    for h in range(0, n_heads, 2)], axis=1)

    for k in range(kt_ref.shape[2] // tq):
        @pl.when(i == k)
        def _(k=k):
            variant(k + 1)


def _fox(q, ktb, vtb, lf, tq):
    b_, l_, _ = q.shape
    nh = FOX_HEADS
    grid = (b_, H_FOX // nh, l_ // tq)
    return pl.pallas_call(
        functools.partial(_fox_kernel, tq=tq), grid=grid,
        in_specs=[pl.BlockSpec((1, tq, nh * HD_FOX), lambda b, p, i: (b, i, p)),
                  pl.BlockSpec((1, nh * HD_FOX, l_), lambda b, p, i: (b, p, 0)),
                  pl.BlockSpec((1, nh * HD_FOX, l_), lambda b, p, i: (b, p, 0)),
                  pl.BlockSpec((1, H_FOX, l_), lambda b, p, i: (b, 0, 0))],
        out_specs=pl.BlockSpec((1, tq, nh * HD_FOX), lambda b, p, i: (b, i, p)),
        out_shape=jax.ShapeDtypeStruct((b_, l_, W_FOX), F32),
        scratch_shapes=[pltpu.VMEM((H_FOX, l_), F32), pltpu.VMEM((nh, tq, l_), F32),
                        pltpu.VMEM((nh, tq, l_), BF16)],
        compiler_params=pltpu.CompilerParams(dimension_semantics=("arbitrary", "arbitrary", "arbitrary")),
        name="fox",
    )(q, ktb, vtb, lf)


def _decode_kernel(pt_ref, qbd_ref, qbd_prev_ref, *refs, n_steps, n_groups):
    del pt_ref
    n = PAGES_PER_STEP
    k_refs, v_refs, lf_refs = refs[0:n], refs[n:2 * n], refs[2 * n:3 * n]
    knew_ref, vnew_ref, lfnew_ref, o_ref, m_scr, l_scr, acc_scr, coff_scr, ctot_scr, s_scr = refs[3 * n:]
    t = pl.program_id(0)
    rows = qbd_ref.shape[1]
    n_q = rows // H_FOX
    slot_w = t % 2
    slot_r = 1 - slot_w
    first_of_seq = (t % n_steps) == 0

    @pl.when(t == 0)
    def _():
        coff_scr[...] = jnp.zeros(coff_scr.shape, F32)
        ctot_scr[...] = jnp.zeros(ctot_scr.shape, F32)
        s_scr[1] = jnp.full(s_scr.shape[1:], -jnp.inf, F32)

    @pl.when((t == 0) | ((t - 1) % n_steps == 0))
    def _():
        m_scr[...] = jnp.full(m_scr.shape, jnp.finfo(F32).min, F32)
        l_scr[...] = jnp.zeros(l_scr.shape, F32)
        acc_scr[...] = jnp.zeros(acc_scr.shape, F32)

    qbd = qbd_ref[0]

    def update(s, v_fn):
        m_prev = m_scr[...]
        m_new = jnp.maximum(m_prev, jnp.max(s, axis=1, keepdims=True))
        alpha = jnp.exp(m_prev - m_new)
        p = jnp.exp(s - m_new)
        l_scr[...] = alpha * l_scr[...] + jnp.sum(p, axis=1, keepdims=True)
        acc_scr[...] = alpha * acc_scr[...] + v_fn(p.astype(BF16))
        m_scr[...] = m_new

    s_prev = s_scr[slot_r]
    m_prev = m_scr[...]
    m_new = jnp.maximum(m_prev, jnp.max(s_prev, axis=1, keepdims=True))
    alpha = jnp.exp(m_prev - m_new)
    p_f32 = jnp.exp(s_prev - m_new)
    l_scr[...] = alpha * l_scr[...] + jnp.sum(p_f32, axis=1, keepdims=True)
    m_scr[...] = m_new
    p = p_f32.astype(BF16)

    lf = jnp.concatenate([r[0, 0] for r in lf_refs], axis=1)
    c = jnp.where(first_of_seq, 0.0, coff_scr[...]) + _cumsum_lanes(lf)
    c_end = c[:, n * PAGE_SIZE - 1:n * PAGE_SIZE]
    live = t < n_groups
    coff_scr[...] = jnp.where(live, c_end, coff_scr[...])
    ctot_scr[...] = jnp.where(live & ((t % n_steps) == n_steps - 1), c_end, ctot_scr[...])
    bias = jnp.concatenate([c] * n_q, axis=0)

    piece = DECODE_PIECE
    pv = None
    for g in range(n // piece):
        cols = slice(g * piece * PAGE_SIZE, (g + 1) * piece * PAGE_SIZE)
        pages = lambda refs: jnp.concatenate(
            [r[0, 0].reshape(W_FOX, PAGE_SIZE).astype(BF16) for r in refs[g * piece:(g + 1) * piece]], axis=1)
        s_scr[slot_w, :, cols] = _dot(qbd, pages(k_refs)) - bias[:, cols]
        d = _dot_nt(p[:, cols], pages(v_refs))
        pv = d if pv is None else pv + d
    acc_scr[...] = alpha * acc_scr[...] + pv

    @pl.when((t > 0) & first_of_seq)
    def _():
        qbd_prev = qbd_prev_ref[0]
        pad = jnp.zeros((PAGE_SIZE - knew_ref.shape[1], W_FOX), F32)
        knew = jnp.concatenate([knew_ref[0], pad], axis=0).astype(BF16)
        vnew = jnp.concatenate([vnew_ref[0], pad], axis=0).astype(BF16)
        cn = ctot_scr[...] + _cumsum_lanes(lfnew_ref[0])
        s2 = _dot_nt(qbd_prev, knew) - jnp.concatenate([cn] * n_q, axis=0)
        qi = lax.broadcasted_iota(jnp.int32, s2.shape, 0) // H_FOX
        ki = lax.broadcasted_iota(jnp.int32, s2.shape, 1)
        s2 = jnp.where(ki <= qi, s2, -jnp.inf)
        update(s2, lambda p: _dot(p, vnew))
        o = acc_scr[...] / l_scr[...]
        hrow = lax.broadcasted_iota(jnp.int32, (H_FOX, W_FOX), 0)
        hcol = lax.broadcasted_iota(jnp.int32, (H_FOX, W_FOX), 1) // HD_FOX
        sel = jnp.where(hrow == hcol, 1.0, 0.0)
        o_ref[0] = jnp.sum(o.reshape(n_q, H_FOX, W_FOX) * sel[None], axis=1)


def _decode(l, page_table, qbd, cache_kt, cache_vt, cache_lft, knew, vnew, lfnew):
    db, n_pages = page_table.shape
    n = PAGES_PER_STEP
    n_steps = n_pages // n
    rows = qbd.shape[1]
    n_q = rows // H_FOX

    assert n_steps >= 2
    n_groups = db * n_steps

    def key_group(t):
        return jnp.minimum(t, n_groups - 1)

    def val_group(t):
        return jnp.maximum(t - 1, 0)

    def page_spec(i, shape, group):
        def index(t, pt):
            g = group(t)
            return (l, pt[g // n_steps, (g % n_steps) * n + i]) + (0,) * len(shape)
        return pl.BlockSpec((1, 1) + shape, index)

    def seq_spec(shape, group):
        return pl.BlockSpec((1,) + shape, lambda t, pt: (group(t) // n_steps,) + (0,) * len(shape))

    in_specs = [seq_spec((rows, W_FOX), key_group), seq_spec((rows, W_FOX), val_group)]
    in_specs += [page_spec(i, (H_FOX, HD_FOX, PAGE_SIZE), key_group) for i in range(n)]
    in_specs += [page_spec(i, (H_FOX, HD_FOX, PAGE_SIZE), val_group) for i in range(n)]
    in_specs += [page_spec(i, (H_FOX, PAGE_SIZE), key_group) for i in range(n)]
    in_specs += [seq_spec((knew.shape[1], W_FOX), val_group), seq_spec((vnew.shape[1], W_FOX), val_group),
                 seq_spec((H_FOX, PAGE_SIZE), val_group)]
    grid_spec = pltpu.PrefetchScalarGridSpec(
        num_scalar_prefetch=1, grid=(n_groups + 1,), in_specs=in_specs,
        out_specs=seq_spec((n_q, W_FOX), val_group),
        scratch_shapes=[pltpu.VMEM((rows, 1), F32), pltpu.VMEM((rows, 1), F32), pltpu.VMEM((rows, W_FOX), F32),
                        pltpu.VMEM((H_FOX, 1), F32), pltpu.VMEM((H_FOX, 1), F32),
                        pltpu.VMEM((2, rows, n * PAGE_SIZE), F32)])
    return pl.pallas_call(
        functools.partial(_decode_kernel, n_steps=n_steps, n_groups=n_groups), grid_spec=grid_spec,
        out_shape=jax.ShapeDtypeStruct((db, n_q, W_FOX), F32),
        compiler_params=pltpu.CompilerParams(dimension_semantics=("arbitrary",)),
        name="decode",
    )(page_table, qbd, qbd, *([cache_kt] * n), *([cache_vt] * n), *([cache_lft] * n), knew, vnew, lfnew)


def _gla_kernel(q_ref, k_ref, la_ref, v_ref, szg_ref, s0_ref, gno_ref, o_ref, sfin_ref, st_scr, *, chunk, n_chunks):
    c_ = chunk
    nb = q_ref.shape[0]
    t = pl.program_id(1)
    row = lax.broadcasted_iota(jnp.int32, (c_, c_), 0)
    col = lax.broadcasted_iota(jnp.int32, (c_, c_), 1)
    tril = col <= row
    tril_b = jnp.where(tril, 1.0, 0.0).astype(BF16)
    lane = lax.broadcasted_iota(jnp.int32, (1, 2 * DK_GLA), 1)
    srow = lax.broadcasted_iota(jnp.int32, (2 * DK_GLA, 2 * DV_GLA), 0) // DK_GLA
    scol = lax.broadcasted_iota(jnp.int32, (2 * DK_GLA, 2 * DV_GLA), 1) // DV_GLA
    blockdiag = srow == scol
    zero = jnp.zeros((DK_GLA, DV_GLA), F32)

    @pl.when(t == 0)
    def _():
        for bi in range(nb):
            for p in range(2):
                top = jnp.concatenate([s0_ref[bi, 2 * p], zero], axis=1)
                bot = jnp.concatenate([zero, s0_ref[bi, 2 * p + 1]], axis=1)
                st_scr[bi, p] = jnp.concatenate([top, bot], axis=0)

    pair_k = [slice(2 * DK_GLA * p, 2 * DK_GLA * (p + 1)) for p in range(2)]
    pair_v = [slice(2 * DV_GLA * p, 2 * DV_GLA * (p + 1)) for p in range(2)]
    units = [(bi, p) for bi in range(nb) for p in range(2)]

    def body(ci, carry):
        off = pl.multiple_of(ci * c_, c_)
        qd, kin, kout, v, bl = {}, {}, {}, {}, {}
        for bi in range(nb):
            la = la_ref[bi, pl.ds(off, c_), :]
            la_hi = la.astype(BF16)
            la_lo = (la - la_hi.astype(F32)).astype(BF16)
            bc = _dot(tril_b, la_hi) + _dot(tril_b, la_lo)
            bl[bi] = bc[c_ - 1:c_, :]
            q = q_ref[bi, pl.ds(off, c_), :]
            k = k_ref[bi, pl.ds(off, c_), :]
            qd[bi] = (q * jnp.exp(bc) * (DK_GLA ** -0.5)).astype(BF16)
            kin[bi] = (k * jnp.exp(-bc)).astype(BF16)
            kout[bi] = (k * jnp.exp(bl[bi] - bc)).astype(BF16)
            v[bi] = v_ref[bi, pl.ds(off, c_), :].astype(BF16)

        o_inter, raw = {}, {}
        for bi, p in units:
            qd_p, kin_p = qd[bi][:, pair_k[p]], kin[bi][:, pair_k[p]]
            o_inter[bi, p] = _dot(qd_p, st_scr[bi, p].astype(BF16))
            head_q = lambda hh: jnp.where((lane // DK_GLA) == hh, qd_p, jnp.zeros_like(qd_p))
            if c_ < GLA_CHUNK:
                a2 = _dot_nt(jnp.concatenate([head_q(0), head_q(1)], axis=0), kin_p)
                raw[bi, p] = [a2[hh * c_:(hh + 1) * c_] for hh in range(2)]
            else:
                raw[bi, p] = [_dot_nt(head_q(hh), kin_p) for hh in range(2)]

        outs = {bi: [] for bi in range(nb)}
        for bi, p in units:
            v_p = v[bi][:, pair_v[p]]
            for hh in range(2):
                a = jnp.where(tril, raw[bi, p][hh], 0.0)
                hs = slice(DV_GLA * hh, DV_GLA * (hh + 1))
                o_h = _dot(a.astype(BF16), v_p[:, hs]) + o_inter[bi, p][:, hs]
                o_h = o_h * lax.rsqrt(jnp.mean(o_h * o_h, axis=-1, keepdims=True) + EPS) * gno_ref[0]
                outs[bi].append(o_h)

        for bi, p in units:
            ds = _dot_tn(kout[bi][:, pair_k[p]], v[bi][:, pair_v[p]])
            dcol = jnp.exp(jnp.broadcast_to(bl[bi][:, pair_k[p]], (2 * DK_GLA, 2 * DK_GLA)).T)
            st_scr[bi, p] = (st_scr[bi, p] * jnp.concatenate([dcol, dcol], axis=1)
                             + jnp.where(blockdiag, ds, 0.0))

        for bi in range(nb):
            o = jnp.concatenate(outs[bi], axis=1) * szg_ref[bi, pl.ds(off, c_), :].astype(F32)
            o_ref[bi, pl.ds(off, c_), :] = o.astype(o_ref.dtype)
        return carry

    lax.fori_loop(0, n_chunks, body, 0)

    @pl.when(t == pl.num_programs(1) - 1)
    def _():
        for bi in range(nb):
            for p in range(2):
                s = st_scr[bi, p]
                sfin_ref[bi, 2 * p] = s[0:DK_GLA, 0:DV_GLA]
                sfin_ref[bi, 2 * p + 1] = s[DK_GLA:2 * DK_GLA, DV_GLA:2 * DV_GLA]


def _gla(l, q, k, la, v, szg, s0, s0_layer, gno, chunk, nb, lt):
    b_, l_, _ = q.shape
    nb = min(nb, b_)
    seq = lambda w: pl.BlockSpec((nb, lt, w), lambda b, t: (b, t, 0))
    if s0_layer is None:
        s0_spec = pl.BlockSpec((nb, H_GLA, DK_GLA, DV_GLA), lambda b, t: (b, 0, 0, 0))
    else:
        s0_spec = pl.BlockSpec((None, nb, H_GLA, DK_GLA, DV_GLA), lambda b, t: (s0_layer, b, 0, 0, 0))
    return pl.pallas_call(
        functools.partial(_gla_kernel, chunk=chunk, n_chunks=lt // chunk), grid=(b_ // nb, l_ // lt),
        in_specs=[seq(WK_GLA), seq(WK_GLA), seq(WK_GLA), seq(WV_GLA), seq(WV_GLA), s0_spec,
                  pl.BlockSpec((1, 1, DV_GLA), lambda b, t: (l, 0, 0))],
        out_specs=(seq(WV_GLA), pl.BlockSpec((nb, H_GLA, DK_GLA, DV_GLA), lambda b, t: (b, 0, 0, 0))),
        out_shape=(jax.ShapeDtypeStruct((b_, l_, WV_GLA), BF16),
                   jax.ShapeDtypeStruct((b_, H_GLA, DK_GLA, DV_GLA), F32)),
        scratch_shapes=[pltpu.VMEM((nb, 2, 2 * DK_GLA, 2 * DV_GLA), F32)],
        compiler_params=pltpu.CompilerParams(dimension_semantics=("arbitrary", "arbitrary")),
        name="gla",
    )(q, k, la, v, szg, s0, gno)


def _out_kernel(x_ref, of_ref, szf_ref, ogz_ref, sgf_ref, sgg_ref, p_ref,
                wbf_ref, wbg_ref, wout_ref, wpg_ref, wple_ref, gple_ref, bpg_ref, y_ref):
    x = x_ref[...]
    yf = _dot((of_ref[...] * szf_ref[...].astype(F32)).astype(BF16), wbf_ref[0])
    yg = _dot(ogz_ref[...], wbg_ref[0])
    m = sgf_ref[...].astype(F32) * yf + sgg_ref[...].astype(F32) * yg
    x1 = x + _dot(m.astype(BF16), wout_ref[0])
    hp = (x1 * lax.rsqrt(jnp.mean(x1 * x1, axis=-1, keepdims=True) + EPS) * gple_ref[0]).astype(BF16)
    gate = jax.nn.sigmoid(_dot(hp, wpg_ref[0]) + bpg_ref[0])
    y_ref[...] = x1 + gate * _dot(p_ref[0].astype(BF16), wple_ref[0])


def _out(l, x, of, szf, ogz, sgf, sgg, p_all, wts, td):
    n_tok = x.shape[0]
    tok = lambda w: pl.BlockSpec((td, w), lambda t: (t, 0))
    lay = lambda *s: pl.BlockSpec((1,) + s, lambda t: (l,) + (0,) * len(s))
    return pl.pallas_call(
        _out_kernel, grid=(n_tok // td,),
        in_specs=[tok(D_MODEL), tok(W_FOX), tok(W_FOX), tok(WV_GLA), tok(D_MODEL), tok(D_MODEL),
                  pl.BlockSpec((1, td, PLE_DIM), lambda t: (l, t, 0)),
                  lay(W_FOX, D_MODEL), lay(WV_GLA, D_MODEL), lay(D_MODEL, D_MODEL), lay(D_MODEL, D_MODEL),
                  lay(PLE_DIM, D_MODEL), lay(1, D_MODEL), lay(1, D_MODEL)],
        out_specs=tok(D_MODEL),
        out_shape=jax.ShapeDtypeStruct((n_tok, D_MODEL), F32),
        compiler_params=pltpu.CompilerParams(dimension_semantics=("arbitrary",)),
        name="out",
    )(x, of, szf, ogz, sgf, sgg, p_all, wts["w_br_fox"], wts["w_br_gla"], wts["w_out"], wts["w_ple_gate"],
      wts["w_ple"], wts["g_ple"], wts["b_ple_gate"])


def _prepare_weights(w_in, b_fox_f, g_pre, g_q, g_k, w_gla_a2, b_gla_a, g_gla_o, b_merge, w_br_fox, w_br_gla,
                     w_out, g_ple, w_ple_gate, b_ple_gate, w_ple):
    depth = w_in.shape[0]
    cols = lambda a, n: w_in[:, :, a:a + n]
    a_pad = jnp.pad(cols(_AG, GLA_RANK), ((0, 0), (0, 0), (0, RANK_PAD - GLA_RANK)))
    w_std = jnp.concatenate([cols(_QF, W_FOX), cols(_ZF, W_FOX), cols(_QG, WK_GLA), cols(_KG, WK_GLA),
                             cols(_VG, WV_GLA), cols(_ZG, WV_GLA), a_pad, cols(_GF, D_MODEL), cols(_GG, D_MODEL)],
                            axis=-1).astype(BF16)
    w_tr = jnp.concatenate([cols(_KF, W_FOX), cols(_VF, W_FOX), cols(_FF, H_FOX)], axis=-1)
    w_tr = jnp.pad(jnp.swapaxes(w_tr, 1, 2), ((0, 0), (0, N_TR - 2 * W_FOX - H_FOX), (0, 0))).astype(BF16)
    head = jnp.arange(W_FOX) // HD_FOX
    return {
        "w_std": w_std, "w_tr": w_tr,
        "bd": (head[:, None] == head[None, :]).astype(BF16),
        "g_pre": g_pre[:, None, :],
        "g_q": jnp.tile(g_q, (1, H_FOX))[:, None, :] * (HD_FOX ** -0.5),
        "g_k": g_k[:, :, None],
        "w_a2": jnp.pad(w_gla_a2, ((0, 0), (0, RANK_PAD - GLA_RANK), (0, 0))).astype(BF16),
        "b_a": b_gla_a[:, None, :],
        "b_ff": b_fox_f[:, :, None],
        "b_mf": b_merge[:, 0:1, :], "b_mg": b_merge[:, 1:2, :],
        "g_gla_o": g_gla_o[:, None, :],
        "w_br_fox": w_br_fox.astype(BF16), "w_br_gla": w_br_gla.astype(BF16), "w_out": w_out.astype(BF16),
        "w_ple_gate": w_ple_gate.astype(BF16), "w_ple": w_ple.astype(BF16),
        "g_ple": g_ple[:, None, :], "b_ple_gate": b_ple_gate[:, None, :],
    }


def kernel(x_prompt, x_sample, p_prompt, p_sample, cache_k, cache_v, cache_logf, state_gla, page_table, w_in, b_fox_f, g_pre, g_q, g_k, w_gla_a2, b_gla_a, g_gla_o, b_merge, w_br_fox, w_br_gla, w_out, g_ple, w_ple_gate, b_ple_gate, w_ple):
    depth = w_in.shape[0]
    bp, lp, _ = x_prompt.shape
    db, ls, _ = x_sample.shape
    n_s = db * ls
    wts = _prepare_weights(w_in, b_fox_f, g_pre, g_q, g_k, w_gla_a2, b_gla_a, g_gla_o, b_merge, w_br_fox,
                           w_br_gla, w_out, g_ple, w_ple_gate, b_ple_gate, w_ple)
    cache_kt = jnp.transpose(cache_k, (0, 1, 3, 4, 2))
    cache_vt = jnp.transpose(cache_v, (0, 1, 3, 4, 2))
    cache_lft = jnp.transpose(cache_logf, (0, 1, 3, 2))
    pp = p_prompt.reshape(depth, bp * lp, PLE_DIM)
    ps = p_sample.reshape(depth, n_s, PLE_DIM)
    head_sel = (jnp.arange(H_FOX)[:, None] == (jnp.arange(W_FOX) // HD_FOX)[None, :]).astype(BF16)
    zero_state = jnp.zeros((bp, H_GLA, DK_GLA, DV_GLA), F32)
    spad = SAMPLE_CHUNK - ls

    xp = x_prompt
    xs = x_sample.reshape(1, n_s, D_MODEL)
    lfp_l, sp_l, ks_l, vs_l, lfs_l, ss_l = ([] for _ in range(6))
    kt_all = jnp.zeros((depth, bp, W_FOX, lp), F32)
    vt_all = jnp.zeros((depth, bp, W_FOX, lp), F32)
    kt_s = jnp.zeros((1, 1, W_FOX, n_s), F32)
    vt_s = jnp.zeros((1, 1, W_FOX, n_s), F32)
    for l in range(depth):
        (q, szf, qg, kg, vg, szg, la, sgf, sgg, kt_all, vt_all, ktb, vtb, lf) = _proj(
            xp, l, wts, PROJ_TILE, BF16, kt_all, vt_all, l)
        of = _fox(q, ktb, vtb, lf, FOX_TILE)
        ogz, s_new = _gla(l, qg, kg, la, vg, szg, zero_state, None, wts["g_gla_o"], GLA_CHUNK,
                          GLA_GROUP, GLA_TOKENS)
        flat = lambda a: a.reshape(bp * lp, a.shape[-1])
        xp = _out(l, flat(xp), flat(of), flat(szf), flat(ogz), flat(sgf), flat(sgg), pp, wts, OUT_TILE)
        xp = xp.reshape(bp, lp, D_MODEL)
        lfp_l.append(lf); sp_l.append(s_new)

        (q, szf, qg, kg, vg, szg, la, sgf, sgg, kt, vt, ktb, vtb, lf) = _proj(
            xs, l, wts, n_s, F32, kt_s, vt_s, 0)
        k_new = kt[0, 0].T.reshape(db, ls, W_FOX)
        v_new = vt[0, 0].T.reshape(db, ls, W_FOX)
        lf_new = jnp.transpose(lf[0].reshape(H_FOX, db, ls), (1, 0, 2))
        qbd = (q.reshape(db, ls, 1, W_FOX) * head_sel[None, None]).reshape(db, ls * H_FOX, W_FOX)
        rpad = (-ls) % 8
        of = _decode(l, page_table, qbd, cache_kt, cache_vt, cache_lft,
                     jnp.pad(k_new, ((0, 0), (0, rpad), (0, 0))), jnp.pad(v_new, ((0, 0), (0, rpad), (0, 0))),
                     jnp.pad(lf_new, ((0, 0), (0, 0), (0, PAGE_SIZE - ls))))
        seqpad = lambda a: jnp.pad(a.reshape(db, ls, a.shape[-1]), ((0, 0), (0, spad), (0, 0)))
        ogz, s_new = _gla(l, seqpad(qg), seqpad(kg), seqpad(la), seqpad(vg), seqpad(szg), state_gla, l,
                          wts["g_gla_o"], SAMPLE_CHUNK, GLA_GROUP, SAMPLE_CHUNK)
        ogz = ogz[:, :ls].reshape(n_s, WV_GLA)
        xs = _out(l, xs[0], of.reshape(n_s, W_FOX), szf[0], ogz, sgf[0], sgg[0], ps, wts, n_s)
        xs = xs.reshape(1, n_s, D_MODEL)
        ks_l.append(k_new.reshape(db, ls, H_FOX, HD_FOX)); vs_l.append(v_new.reshape(db, ls, H_FOX, HD_FOX))
        lfs_l.append(jnp.transpose(lf_new, (0, 2, 1))); ss_l.append(s_new)

    def heads_out(a):
        return jnp.transpose(a.reshape(depth, bp, H_FOX, HD_FOX, lp), (0, 1, 4, 2, 3))

    return (xp, xs.reshape(db, ls, D_MODEL), heads_out(kt_all), heads_out(vt_all),
            jnp.transpose(jnp.stack(lfp_l), (0, 1, 3, 2)), jnp.stack(sp_l),
            jnp.stack(ks_l), jnp.stack(vs_l), jnp.stack(lfs_l), jnp.stack(ss_l))
```

```python
import functools

import jax
import jax.numpy as jnp
from jax import lax
from jax.experimental import pallas as pl
from jax.experimental.pallas import tpu as pltpu

F32 = jnp.float32
BF16 = jnp.bfloat16

D_MODEL = 1024
PLE_DIM = 256
HD_FOX = 64
W_FOX = 512
H_FOX = 8
H_GLA = 4
WK_GLA = 256
WV_GLA = 512
DK_GLA = 64
DV_GLA = 128
GLA_RANK = 16
GLA_CHUNK = 64
PAGE_SIZE = 128
EPS = 1e-6
LANES = 128
LOG2E = 1.4426950408889634

_SIZES = (W_FOX, W_FOX, W_FOX, W_FOX, H_FOX, WK_GLA, WK_GLA, WV_GLA, WV_GLA, GLA_RANK, D_MODEL, D_MODEL)
_OFF = [0]
for _s in _SIZES:
    _OFF.append(_OFF[-1] + _s)
(_QF, _KF, _VF, _ZF, _FF, _QG, _KG, _VG, _ZG, _AG, _GF, _GG) = _OFF[:-1]

RANK_PAD = 128
_C_QF, _C_ZF, _C_QG, _C_KG, _C_VG, _C_ZG, _C_AG, _C_GF, _C_GG, N_STD = (
    0, 512, 1024, 1280, 1536, 2048, 2560, 2688, 3712, 4736)
N_TR = 1040

PAGES_PER_STEP = 16
DECODE_PIECE = 4
SAMPLE_CHUNK = 16
PROJ_TILE = 256
FOX_TILE = 512
FOX_ROWS = 256
FOX_HEADS = 4
OUT_TILE = 512
GLA_GROUP = 8
GLA_TOKENS = 256


def _log_sigmoid(x):
    return jnp.minimum(x, 0.0) - jnp.log1p(jnp.exp(-jnp.abs(x)))


def _silu(x):
    return x * jax.nn.sigmoid(x)


def _dot(a, b):
    return jnp.dot(a, b, preferred_element_type=F32)


def _dot_nt(a, b):
    return lax.dot_general(a, b, (((1,), (1,)), ((), ())), preferred_element_type=F32)


def _dot_tn(a, b):
    return lax.dot_general(a, b, (((0,), (0,)), ((), ())), preferred_element_type=F32)


def _cumsum_lanes(x):
    n = x.shape[-1]
    lane = lax.broadcasted_iota(jnp.int32, x.shape, x.ndim - 1)
    k = 1
    while k < n:
        x = x + jnp.where(lane >= k, pltpu.roll(x, k, axis=x.ndim - 1), 0.0)
        k *= 2
    return x


def _proj_kernel(x_ref, gpre_ref, wstd_ref, wtr_ref, bd_ref, gq_ref, gk_ref, wa2_ref, ba_ref, bff_ref,
                 bmf_ref, bmg_ref, kt_all_ref, vt_all_ref,
                 q_ref, szf_ref, qg_ref, kg_ref, vg_ref, szg_ref, la_ref, sgf_ref, sgg_ref,
                 kt_ref, vt_ref, ktb_ref, vtb_ref, lf_ref):
    del kt_all_ref, vt_all_ref
    x = x_ref[0]
    ta = x.shape[0]
    h = (x * lax.rsqrt(jnp.mean(x * x, axis=-1, keepdims=True) + EPS) * gpre_ref[0]).astype(BF16)

    def mm(lo, hi):
        return _dot(h, wstd_ref[0, :, lo:hi])

    qf = mm(_C_QF, _C_ZF)
    ss = _dot((qf * qf).astype(BF16), bd_ref[...]) * (1.0 / HD_FOX)
    q_ref[0] = (qf * lax.rsqrt(ss + EPS) * gq_ref[0]).astype(BF16)
    szf_ref[0] = _silu(mm(_C_ZF, _C_QG)).astype(BF16)
    qg_ref[0] = mm(_C_QG, _C_KG)
    kg_ref[0] = mm(_C_KG, _C_VG)
    vg_ref[0] = mm(_C_VG, _C_ZG).astype(vg_ref.dtype)
    szg_ref[0] = _silu(mm(_C_ZG, _C_AG)).astype(szg_ref.dtype)
    ag = mm(_C_AG, _C_GF)
    la_ref[0] = _log_sigmoid(_dot(ag.astype(BF16), wa2_ref[0]) + ba_ref[0]) * (1.0 / 16.0)
    sgf_ref[0] = jax.nn.sigmoid(mm(_C_GF, _C_GG) + bmf_ref[0]).astype(BF16)
    sgg_ref[0] = jax.nn.sigmoid(mm(_C_GG, N_STD) + bmg_ref[0]).astype(BF16)

    r = _dot_nt(wtr_ref[0], h)
    kt = r[0:W_FOX].reshape(H_FOX, HD_FOX, ta)
    kt = kt * lax.rsqrt(jnp.mean(kt * kt, axis=1, keepdims=True) + EPS) * gk_ref[0][None]
    kt = kt.reshape(W_FOX, ta)
    kt_ref[0, 0] = kt
    ktb_ref[0] = kt.astype(BF16)
    vt = r[W_FOX:2 * W_FOX]
    vt_ref[0, 0] = vt
    vtb_ref[0] = vt.astype(BF16)
    lf_ref[0] = _log_sigmoid(r[2 * W_FOX:2 * W_FOX + H_FOX] + bff_ref[0])


def _proj(x, l, wts, ta, gla_dtype, kt_all, vt_all, l_out):
    b_, l_, _ = x.shape
    grid = (b_, l_ // ta)
    tok = lambda w: pl.BlockSpec((1, ta, w), lambda b, t: (b, t, 0))
    feat = lambda r: pl.BlockSpec((1, r, ta), lambda b, t: (b, 0, t))
    slot = pl.BlockSpec((1, 1, W_FOX, ta), lambda b, t: (l_out, b, 0, t))
    lay = lambda *s: pl.BlockSpec((1,) + s, lambda b, t: (l,) + (0,) * len(s))
    const = lambda *s: pl.BlockSpec(s, lambda b, t: (0,) * len(s))
    hbm = pl.BlockSpec(memory_space=pl.ANY)
    sds = jax.ShapeDtypeStruct
    out_shape = (
        sds((b_, l_, W_FOX), BF16), sds((b_, l_, W_FOX), BF16),
        sds((b_, l_, WK_GLA), F32), sds((b_, l_, WK_GLA), F32),
        sds((b_, l_, WV_GLA), gla_dtype), sds((b_, l_, WV_GLA), gla_dtype),
        sds((b_, l_, WK_GLA), F32),
        sds((b_, l_, D_MODEL), BF16), sds((b_, l_, D_MODEL), BF16),
        sds(kt_all.shape, F32), sds(vt_all.shape, F32),
        sds((b_, W_FOX, l_), BF16), sds((b_, W_FOX, l_), BF16),
        sds((b_, H_FOX, l_), F32),
    )
    out_specs = (tok(W_FOX), tok(W_FOX), tok(WK_GLA), tok(WK_GLA), tok(WV_GLA), tok(WV_GLA), tok(WK_GLA),
                 tok(D_MODEL), tok(D_MODEL), slot, slot, feat(W_FOX), feat(W_FOX), feat(H_FOX))
    in_specs = [tok(D_MODEL), lay(1, D_MODEL), lay(D_MODEL, N_STD), lay(N_TR, D_MODEL), const(W_FOX, W_FOX),
                lay(1, W_FOX), lay(HD_FOX, 1), lay(RANK_PAD, WK_GLA), lay(1, WK_GLA), lay(H_FOX, 1),
                lay(1, D_MODEL), lay(1, D_MODEL), hbm, hbm]
    return pl.pallas_call(
        _proj_kernel, grid=grid, in_specs=in_specs, out_specs=out_specs, out_shape=out_shape,
        input_output_aliases={12: 9, 13: 10},
        compiler_params=pltpu.CompilerParams(dimension_semantics=("arbitrary", "arbitrary")),
        name="proj",
    )(x, wts["g_pre"], wts["w_std"], wts["w_tr"], wts["bd"], wts["g_q"], wts["g_k"], wts["w_a2"], wts["b_a"],
      wts["b_ff"], wts["b_mf"], wts["b_mg"], kt_all, vt_all)


def _fox_kernel(q_ref, kt_ref, vt_ref, lf_ref, o_ref, c_scr, s_scr, p_scr, *, tq):
    pr = pl.program_id(1)
    i = pl.program_id(2)
    n_heads = q_ref.shape[2] // HD_FOX

    @pl.when((pr == 0) & (i == 0))
    def _():
        c_scr[...] = _cumsum_lanes(lf_ref[0]) * LOG2E

    lane = lax.broadcasted_iota(jnp.int32, (1, 2 * HD_FOX), 1)
    pair_rows = [slice(2 * HD_FOX * (h // 2), 2 * HD_FOX * (h // 2 + 1)) for h in range(n_heads)]
    qh = []
    for h in range(n_heads):
        q_pair = q_ref[0, :, pair_rows[h]]
        qh.append(jnp.where((lane // HD_FOX) == h % 2, q_pair, jnp.zeros_like(q_pair)))
    rg = FOX_ROWS
    n_rg = tq // rg
    col = lax.broadcasted_iota(jnp.int32, (rg, tq), 1)
    rows = [lax.broadcasted_iota(jnp.int32, (rg, tq), 0) + r * rg for r in range(n_rg)]
    n_lane_tiles = tq // LANES

    def variant(n_blocks):
        crow = [c_scr[pl.ds(n_heads * pr + hh, 1), :] for hh in range(n_heads)]
        mx, m_b, lacc, acc = {}, {}, {}, {}

        def logits(hh, j, r):
            ks = slice(j * tq, (j + 1) * tq)
            rs = slice(r * rg, (r + 1) * rg)
            s = _dot(qh[hh][rs], kt_ref[0, pair_rows[hh], ks]) - crow[hh][:, ks]
            if j == n_blocks - 1:
                s = jnp.where(col <= rows[r], s, -jnp.inf)
            s_scr[hh, rs, ks] = s
            for c in range(n_lane_tiles):
                sc = s[:, c * LANES:(c + 1) * LANES]
                mx[hh, r] = sc if (hh, r) not in mx else jnp.maximum(mx[hh, r], sc)

        def row_max(hh):
            for r in range(n_rg):
                m_b[hh, r] = jnp.broadcast_to(jnp.max(mx[hh, r], axis=1, keepdims=True), (rg, LANES))

        def exps(hh, j, r):
            rs = slice(r * rg, (r + 1) * rg)
            for c in range(j * n_lane_tiles, (j + 1) * n_lane_tiles):
                cs = slice(c * LANES, (c + 1) * LANES)
                pc = jnp.exp2(s_scr[hh, rs, cs] - m_b[hh, r])
                lacc[hh, r] = pc if (hh, r) not in lacc else lacc[hh, r] + pc
                p_scr[hh, rs, cs] = pc.astype(BF16)

        def values(hh, j, r):
            ks = slice(j * tq, (j + 1) * tq)
            rs = slice(r * rg, (r + 1) * rg)
            pv = _dot_nt(p_scr[hh, rs, ks], vt_ref[0, pair_rows[hh], ks])
            acc[hh, r] = pv if (hh, r) not in acc else acc[hh, r] + pv

        items = [(j, r) for j in range(n_blocks) for r in range(n_rg)]
        for step in range(n_heads + 2):
            for j, r in items:
                if step < n_heads:
                    logits(step, j, r)
                if 0 <= step - 1 < n_heads:
                    exps(step - 1, j, r)
                if 0 <= step - 2 < n_heads:
                    values(step - 2, j, r)
            if step < n_heads:
                row_max(step)
        o = [jnp.concatenate([acc[hh, r] / jnp.sum(lacc[hh, r], axis=1, keepdims=True) for r in range(n_rg)],
                             axis=0) for hh in range(n_heads)]
        o_ref[0] = jnp.concatenate([jnp.where((lane // HD_FOX) == 0, o[h], o[h + 1])
                                    for h in range(0, n_heads, 2)], axis=1)

    for k in range(kt_ref.shape[2] // tq):
        @pl.when(i == k)
        def _(k=k):
            variant(k + 1)


def _fox(q, ktb, vtb, lf, tq):
    b_, l_, _ = q.shape
    nh = FOX_HEADS
    grid = (b_, H_FOX // nh, l_ // tq)
    return pl.pallas_call(
        functools.partial(_fox_kernel, tq=tq), grid=grid,
        in_specs=[pl.BlockSpec((1, tq, nh * HD_FOX), lambda b, p, i: (b, i, p)),
                  pl.BlockSpec((1, nh * HD_FOX, l_), lambda b, p, i: (b, p, 0)),
                  pl.BlockSpec((1, nh * HD_FOX, l_), lambda b, p, i: (b, p, 0)),
                  pl.BlockSpec((1, H_FOX, l_), lambda b, p, i: (b, 0, 0))],
        out_specs=pl.BlockSpec((1, tq, nh * HD_FOX), lambda b, p, i: (b, i, p)),
        out_shape=jax.ShapeDtypeStruct((b_, l_, W_FOX), F32),
        scratch_shapes=[pltpu.VMEM((H_FOX, l_), F32), pltpu.VMEM((nh, tq, l_), F32),
                        pltpu.VMEM((nh, tq, l_), BF16)],
        compiler_params=pltpu.CompilerParams(dimension_semantics=("arbitrary", "arbitrary", "arbitrary")),
        name="fox",
    )(q, ktb, vtb, lf)


def _decode_kernel(pk_ref, pv_ref, sk_ref, sv_ref, qbd_ref, qbd_prev_ref, *refs, n_steps, n_groups):
    del pk_ref, pv_ref, sk_ref, sv_ref
    n = PAGES_PER_STEP
    k_refs, v_refs, lf_refs = refs[0:n], refs[n:2 * n], refs[2 * n:3 * n]
    knew_ref, vnew_ref, lfnew_ref, o_ref, m_scr, l_scr, acc_scr, coff_scr, ctot_scr, s_scr = refs[3 * n:]
    t = pl.program_id(0)
    rows = qbd_ref.shape[1]
    n_q = rows // H_FOX
    slot_w = t % 2
    slot_r = 1 - slot_w
    first_of_seq = (t % n_steps) == 0

    @pl.when(t == 0)
    def _():
        coff_scr[...] = jnp.zeros(coff_scr.shape, F32)
        ctot_scr[...] = jnp.zeros(ctot_scr.shape, F32)
        s_scr[1] = jnp.full(s_scr.shape[1:], -jnp.inf, F32)

    @pl.when((t == 0) | ((t - 1) % n_steps == 0))
    def _():
        m_scr[...] = jnp.full(m_scr.shape, jnp.finfo(F32).min, F32)
        l_scr[...] = jnp.zeros(l_scr.shape, F32)
        acc_scr[...] = jnp.zeros(acc_scr.shape, F32)

    qbd = qbd_ref[0]

    def update(s, v_fn):
        m_prev = m_scr[...]
        m_new = jnp.maximum(m_prev, jnp.max(s, axis=1, keepdims=True))
        alpha = jnp.exp2(m_prev - m_new)
        p = jnp.exp2(s - m_new)
        l_scr[...] = alpha * l_scr[...] + jnp.sum(p, axis=1, keepdims=True)
        acc_scr[...] = alpha * acc_scr[...] + v_fn(p.astype(BF16))
        m_scr[...] = m_new

    s_prev = s_scr[slot_r]
    m_prev = m_scr[...]
    m_new = jnp.maximum(m_prev, jnp.max(s_prev, axis=1, keepdims=True))
    alpha = jnp.exp2(m_prev - m_new)
    p_f32 = jnp.exp2(s_prev - m_new)
    l_scr[...] = alpha * l_scr[...] + jnp.sum(p_f32, axis=1, keepdims=True)
    m_scr[...] = m_new
    p = p_f32.astype(BF16)

    lf = jnp.concatenate([r[0, 0] for r in lf_refs], axis=1)
    c = jnp.where(first_of_seq, 0.0, coff_scr[...]) + _cumsum_lanes(lf)
    c_end = c[:, n * PAGE_SIZE - 1:n * PAGE_SIZE]
    live = t < n_groups
    coff_scr[...] = jnp.where(live, c_end, coff_scr[...])
    ctot_scr[...] = jnp.where(live & ((t % n_steps) == n_steps - 1), c_end, ctot_scr[...])
    bias = jnp.concatenate([c * LOG2E] * n_q, axis=0)

    piece = DECODE_PIECE
    pv = None
    for g in range(n // piece):
        cols = slice(g * piece * PAGE_SIZE, (g + 1) * piece * PAGE_SIZE)
        pages = lambda refs: jnp.concatenate(
            [r[0, 0].reshape(W_FOX, PAGE_SIZE).astype(BF16) for r in refs[g * piece:(g + 1) * piece]], axis=1)
        s_scr[slot_w, :, cols] = _dot(qbd, pages(k_refs)) - bias[:, cols]
        d = _dot_nt(p[:, cols], pages(v_refs))
        pv = d if pv is None else pv + d
    acc_scr[...] = alpha * acc_scr[...] + pv

    @pl.when((t > 0) & first_of_seq)
    def _():
        qbd_prev = qbd_prev_ref[0]
        pad = jnp.zeros((PAGE_SIZE - knew_ref.shape[1], W_FOX), F32)
        knew = jnp.concatenate([knew_ref[0], pad], axis=0).astype(BF16)
        vnew = jnp.concatenate([vnew_ref[0], pad], axis=0).astype(BF16)
        cn = ctot_scr[...] + _cumsum_lanes(lfnew_ref[0])
        s2 = _dot_nt(qbd_prev, knew) - jnp.concatenate([cn * LOG2E] * n_q, axis=0)
        qi = lax.broadcasted_iota(jnp.int32, s2.shape, 0) // H_FOX
        ki = lax.broadcasted_iota(jnp.int32, s2.shape, 1)
        s2 = jnp.where(ki <= qi, s2, -jnp.inf)
        update(s2, lambda p: _dot(p, vnew))
        o = acc_scr[...] / l_scr[...]
        hrow = lax.broadcasted_iota(jnp.int32, (H_FOX, W_FOX), 0)
        hcol = lax.broadcasted_iota(jnp.int32, (H_FOX, W_FOX), 1) // HD_FOX
        sel = jnp.where(hrow == hcol, 1.0, 0.0)
        o_ref[0] = jnp.sum(o.reshape(n_q, H_FOX, W_FOX) * sel[None], axis=1)


def _decode(l, page_table, qbd, cache_kt, cache_vt, cache_lft, knew, vnew, lfnew):
    db, n_pages = page_table.shape
    n = PAGES_PER_STEP
    n_steps = n_pages // n
    rows = qbd.shape[1]
    n_q = rows // H_FOX

    assert n_steps >= 2
    n_groups = db * n_steps

    steps = jnp.arange(n_groups + 1, dtype=jnp.int32)
    groups = (jnp.minimum(steps, n_groups - 1), jnp.maximum(steps - 1, 0))
    flat_pages = page_table.reshape(-1)
    page_tabs = [flat_pages[(g[:, None] * n + jnp.arange(n, dtype=jnp.int32)[None, :]).reshape(-1)]
                 for g in groups]
    seq_tabs = [g // n_steps for g in groups]
    KEY, VAL = 0, 1

    def page_spec(i, shape, side):
        return pl.BlockSpec((1, 1) + shape,
                            lambda t, pk, pv, sk, sv: (l, (pk, pv)[side][t * n + i]) + (0,) * len(shape))

    def seq_spec(shape, side):
        return pl.BlockSpec((1,) + shape, lambda t, pk, pv, sk, sv: ((sk, sv)[side][t],) + (0,) * len(shape))

    in_specs = [seq_spec((rows, W_FOX), KEY), seq_spec((rows, W_FOX), VAL)]
    in_specs += [page_spec(i, (H_FOX, HD_FOX, PAGE_SIZE), KEY) for i in range(n)]
    in_specs += [page_spec(i, (H_FOX, HD_FOX, PAGE_SIZE), VAL) for i in range(n)]
    in_specs += [page_spec(i, (H_FOX, PAGE_SIZE), KEY) for i in range(n)]
    in_specs += [seq_spec((knew.shape[1], W_FOX), VAL), seq_spec((vnew.shape[1], W_FOX), VAL),
                 seq_spec((H_FOX, PAGE_SIZE), VAL)]
    grid_spec = pltpu.PrefetchScalarGridSpec(
        num_scalar_prefetch=4, grid=(n_groups + 1,), in_specs=in_specs,
        out_specs=seq_spec((n_q, W_FOX), VAL),
        scratch_shapes=[pltpu.VMEM((rows, 1), F32), pltpu.VMEM((rows, 1), F32), pltpu.VMEM((rows, W_FOX), F32),
                        pltpu.VMEM((H_FOX, 1), F32), pltpu.VMEM((H_FOX, 1), F32),
                        pltpu.VMEM((2, rows, n * PAGE_SIZE), F32)])
    return pl.pallas_call(
        functools.partial(_decode_kernel, n_steps=n_steps, n_groups=n_groups), grid_spec=grid_spec,
        out_shape=jax.ShapeDtypeStruct((db, n_q, W_FOX), F32),
        compiler_params=pltpu.CompilerParams(dimension_semantics=("arbitrary",)),
        name="decode",
    )(*page_tabs, *seq_tabs, qbd, qbd, *([cache_kt] * n), *([cache_vt] * n), *([cache_lft] * n), knew, vnew, lfnew)


def _gla_kernel(q_ref, k_ref, la_ref, v_ref, szg_ref, s0_ref, gno_ref, o_ref, sfin_ref, st_scr, *, chunk, n_chunks):
    c_ = chunk
    nb = q_ref.shape[0]
    t = pl.program_id(1)
    row = lax.broadcasted_iota(jnp.int32, (c_, c_), 0)
    col = lax.broadcasted_iota(jnp.int32, (c_, c_), 1)
    tril = col <= row
    tril_b = jnp.where(tril, 1.0, 0.0).astype(BF16)
    lane = lax.broadcasted_iota(jnp.int32, (1, 2 * DK_GLA), 1)
    srow = lax.broadcasted_iota(jnp.int32, (2 * DK_GLA, 2 * DV_GLA), 0) // DK_GLA
    scol = lax.broadcasted_iota(jnp.int32, (2 * DK_GLA, 2 * DV_GLA), 1) // DV_GLA
    blockdiag = srow == scol
    zero = jnp.zeros((DK_GLA, DV_GLA), F32)

    @pl.when(t == 0)
    def _():
        for bi in range(nb):
            for p in range(2):
                top = jnp.concatenate([s0_ref[bi, 2 * p], zero], axis=1)
                bot = jnp.concatenate([zero, s0_ref[bi, 2 * p + 1]], axis=1)
                st_scr[bi, p] = jnp.concatenate([top, bot], axis=0)

    pair_k = [slice(2 * DK_GLA * p, 2 * DK_GLA * (p + 1)) for p in range(2)]
    pair_v = [slice(2 * DV_GLA * p, 2 * DV_GLA * (p + 1)) for p in range(2)]
    units = [(bi, p) for bi in range(nb) for p in range(2)]

    def body(ci, carry):
        off = pl.multiple_of(ci * c_, c_)
        qd, kin, kout, v, bl = {}, {}, {}, {}, {}
        for bi in range(nb):
            la = la_ref[bi, pl.ds(off, c_), :]
            la_hi = la.astype(BF16)
            la_lo = (la - la_hi.astype(F32)).astype(BF16)
            bc = _dot(tril_b, la_hi) + _dot(tril_b, la_lo)
            bl[bi] = bc[c_ - 1:c_, :]
            q = q_ref[bi, pl.ds(off, c_), :]
            k = k_ref[bi, pl.ds(off, c_), :]
            qd[bi] = (q * jnp.exp(bc) * (DK_GLA ** -0.5)).astype(BF16)
            kin[bi] = (k * jnp.exp(-bc)).astype(BF16)
            kout[bi] = (k * jnp.exp(bl[bi] - bc)).astype(BF16)
            v[bi] = v_ref[bi, pl.ds(off, c_), :].astype(BF16)

        o_inter, raw = {}, {}
        for bi, p in units:
            qd_p, kin_p = qd[bi][:, pair_k[p]], kin[bi][:, pair_k[p]]
            o_inter[bi, p] = _dot(qd_p, st_scr[bi, p].astype(BF16))
            head_q = lambda hh: jnp.where((lane // DK_GLA) == hh, qd_p, jnp.zeros_like(qd_p))
            if c_ < GLA_CHUNK:
                a2 = _dot_nt(jnp.concatenate([head_q(0), head_q(1)], axis=0), kin_p)
                raw[bi, p] = [a2[hh * c_:(hh + 1) * c_] for hh in range(2)]
            else:
                raw[bi, p] = [_dot_nt(head_q(hh), kin_p) for hh in range(2)]

        outs = {bi: [] for bi in range(nb)}
        for bi, p in units:
            v_p = v[bi][:, pair_v[p]]
            for hh in range(2):
                a = jnp.where(tril, raw[bi, p][hh], 0.0)
                hs = slice(DV_GLA * hh, DV_GLA * (hh + 1))
                o_h = _dot(a.astype(BF16), v_p[:, hs]) + o_inter[bi, p][:, hs]
                o_h = o_h * lax.rsqrt(jnp.mean(o_h * o_h, axis=-1, keepdims=True) + EPS) * gno_ref[0]
                outs[bi].append(o_h)

        for bi, p in units:
            ds = _dot_tn(kout[bi][:, pair_k[p]], v[bi][:, pair_v[p]])
            dcol = jnp.exp(jnp.broadcast_to(bl[bi][:, pair_k[p]], (2 * DK_GLA, 2 * DK_GLA)).T)
            st_scr[bi, p] = (st_scr[bi, p] * jnp.concatenate([dcol, dcol], axis=1)
                             + jnp.where(blockdiag, ds, 0.0))

        for bi in range(nb):
            o = jnp.concatenate(outs[bi], axis=1) * szg_ref[bi, pl.ds(off, c_), :].astype(F32)
            o_ref[bi, pl.ds(off, c_), :] = o.astype(o_ref.dtype)
        return carry

    lax.fori_loop(0, n_chunks, body, 0)

    @pl.when(t == pl.num_programs(1) - 1)
    def _():
        for bi in range(nb):
            for p in range(2):
                s = st_scr[bi, p]
                sfin_ref[bi, 2 * p] = s[0:DK_GLA, 0:DV_GLA]
                sfin_ref[bi, 2 * p + 1] = s[DK_GLA:2 * DK_GLA, DV_GLA:2 * DV_GLA]


def _gla(l, q, k, la, v, szg, s0, s0_layer, gno, chunk, nb, lt):
    b_, l_, _ = q.shape
    nb = min(nb, b_)
    seq = lambda w: pl.BlockSpec((nb, lt, w), lambda b, t: (b, t, 0))
    if s0_layer is None:
        s0_spec = pl.BlockSpec((nb, H_GLA, DK_GLA, DV_GLA), lambda b, t: (b, 0, 0, 0))
    else:
        s0_spec = pl.BlockSpec((None, nb, H_GLA, DK_GLA, DV_GLA), lambda b, t: (s0_layer, b, 0, 0, 0))
    return pl.pallas_call(
        functools.partial(_gla_kernel, chunk=chunk, n_chunks=lt // chunk), grid=(b_ // nb, l_ // lt),
        in_specs=[seq(WK_GLA), seq(WK_GLA), seq(WK_GLA), seq(WV_GLA), seq(WV_GLA), s0_spec,
                  pl.BlockSpec((1, 1, DV_GLA), lambda b, t: (l, 0, 0))],
        out_specs=(seq(WV_GLA), pl.BlockSpec((nb, H_GLA, DK_GLA, DV_GLA), lambda b, t: (b, 0, 0, 0))),
        out_shape=(jax.ShapeDtypeStruct((b_, l_, WV_GLA), BF16),
                   jax.ShapeDtypeStruct((b_, H_GLA, DK_GLA, DV_GLA), F32)),
        scratch_shapes=[pltpu.VMEM((nb, 2, 2 * DK_GLA, 2 * DV_GLA), F32)],
        compiler_params=pltpu.CompilerParams(dimension_semantics=("arbitrary", "arbitrary")),
        name="gla",
    )(q, k, la, v, szg, s0, gno)


def _out_kernel(x_ref, of_ref, szf_ref, ogz_ref, sgf_ref, sgg_ref, p_ref,
                wbf_ref, wbg_ref, wout_ref, wpg_ref, wple_ref, gple_ref, bpg_ref, y_ref):
    x = x_ref[...]
    yf = _dot((of_ref[...] * szf_ref[...].astype(F32)).astype(BF16), wbf_ref[0])
    yg = _dot(ogz_ref[...], wbg_ref[0])
    m = sgf_ref[...].astype(F32) * yf + sgg_ref[...].astype(F32) * yg
    x1 = x + _dot(m.astype(BF16), wout_ref[0])
    hp = (x1 * lax.rsqrt(jnp.mean(x1 * x1, axis=-1, keepdims=True) + EPS) * gple_ref[0]).astype(BF16)
    gate = jax.nn.sigmoid(_dot(hp, wpg_ref[0]) + bpg_ref[0])
    y_ref[...] = x1 + gate * _dot(p_ref[0].astype(BF16), wple_ref[0])


def _out(l, x, of, szf, ogz, sgf, sgg, p_all, wts, td):
    n_tok = x.shape[0]
    tok = lambda w: pl.BlockSpec((td, w), lambda t: (t, 0))
    lay = lambda *s: pl.BlockSpec((1,) + s, lambda t: (l,) + (0,) * len(s))
    return pl.pallas_call(
        _out_kernel, grid=(n_tok // td,),
        in_specs=[tok(D_MODEL), tok(W_FOX), tok(W_FOX), tok(WV_GLA), tok(D_MODEL), tok(D_MODEL),
                  pl.BlockSpec((1, td, PLE_DIM), lambda t: (l, t, 0)),
                  lay(W_FOX, D_MODEL), lay(WV_GLA, D_MODEL), lay(D_MODEL, D_MODEL), lay(D_MODEL, D_MODEL),
                  lay(PLE_DIM, D_MODEL), lay(1, D_MODEL), lay(1, D_MODEL)],
        out_specs=tok(D_MODEL),
        out_shape=jax.ShapeDtypeStruct((n_tok, D_MODEL), F32),
        compiler_params=pltpu.CompilerParams(dimension_semantics=("arbitrary",)),
        name="out",
    )(x, of, szf, ogz, sgf, sgg, p_all, wts["w_br_fox"], wts["w_br_gla"], wts["w_out"], wts["w_ple_gate"],
      wts["w_ple"], wts["g_ple"], wts["b_ple_gate"])


def _prepare_weights(w_in, b_fox_f, g_pre, g_q, g_k, w_gla_a2, b_gla_a, g_gla_o, b_merge, w_br_fox, w_br_gla,
                     w_out, g_ple, w_ple_gate, b_ple_gate, w_ple):
    depth = w_in.shape[0]
    cols = lambda a, n: w_in[:, :, a:a + n]
    a_pad = jnp.pad(cols(_AG, GLA_RANK), ((0, 0), (0, 0), (0, RANK_PAD - GLA_RANK)))
    w_std = jnp.concatenate([cols(_QF, W_FOX), cols(_ZF, W_FOX), cols(_QG, WK_GLA), cols(_KG, WK_GLA),
                             cols(_VG, WV_GLA), cols(_ZG, WV_GLA), a_pad, cols(_GF, D_MODEL), cols(_GG, D_MODEL)],
                            axis=-1).astype(BF16)
    w_tr = jnp.concatenate([cols(_KF, W_FOX), cols(_VF, W_FOX), cols(_FF, H_FOX)], axis=-1)
    w_tr = jnp.pad(jnp.swapaxes(w_tr, 1, 2), ((0, 0), (0, N_TR - 2 * W_FOX - H_FOX), (0, 0))).astype(BF16)
    head = jnp.arange(W_FOX) // HD_FOX
    return {
        "w_std": w_std, "w_tr": w_tr,
        "bd": (head[:, None] == head[None, :]).astype(BF16),
        "g_pre": g_pre[:, None, :],
        "g_q": jnp.tile(g_q, (1, H_FOX))[:, None, :] * (LOG2E * HD_FOX ** -0.5),
        "g_k": g_k[:, :, None],
        "w_a2": jnp.pad(w_gla_a2, ((0, 0), (0, RANK_PAD - GLA_RANK), (0, 0))).astype(BF16),
        "b_a": b_gla_a[:, None, :],
        "b_ff": b_fox_f[:, :, None],
        "b_mf": b_merge[:, 0:1, :], "b_mg": b_merge[:, 1:2, :],
        "g_gla_o": g_gla_o[:, None, :],
        "w_br_fox": w_br_fox.astype(BF16), "w_br_gla": w_br_gla.astype(BF16), "w_out": w_out.astype(BF16),
        "w_ple_gate": w_ple_gate.astype(BF16), "w_ple": w_ple.astype(BF16),
        "g_ple": g_ple[:, None, :], "b_ple_gate": b_ple_gate[:, None, :],
    }


def kernel(x_prompt, x_sample, p_prompt, p_sample, cache_k, cache_v, cache_logf, state_gla, page_table, w_in, b_fox_f, g_pre, g_q, g_k, w_gla_a2, b_gla_a, g_gla_o, b_merge, w_br_fox, w_br_gla, w_out, g_ple, w_ple_gate, b_ple_gate, w_ple):
    depth = w_in.shape[0]
    bp, lp, _ = x_prompt.shape
    db, ls, _ = x_sample.shape
    n_s = db * ls
    wts = _prepare_weights(w_in, b_fox_f, g_pre, g_q, g_k, w_gla_a2, b_gla_a, g_gla_o, b_merge, w_br_fox,
                           w_br_gla, w_out, g_ple, w_ple_gate, b_ple_gate, w_ple)
    cache_kt = jnp.transpose(cache_k, (0, 1, 3, 4, 2))
    cache_vt = jnp.transpose(cache_v, (0, 1, 3, 4, 2))
    cache_lft = jnp.transpose(cache_logf, (0, 1, 3, 2))
    pp = p_prompt.reshape(depth, bp * lp, PLE_DIM)
    ps = p_sample.reshape(depth, n_s, PLE_DIM)
    head_sel = (jnp.arange(H_FOX)[:, None] == (jnp.arange(W_FOX) // HD_FOX)[None, :]).astype(BF16)
    zero_state = jnp.zeros((bp, H_GLA, DK_GLA, DV_GLA), F32)
    spad = SAMPLE_CHUNK - ls

    xp = x_prompt
    xs = x_sample.reshape(1, n_s, D_MODEL)
    lfp_l, sp_l, ks_l, vs_l, lfs_l, ss_l = ([] for _ in range(6))
    kt_all = jnp.zeros((depth, bp, W_FOX, lp), F32)
    vt_all = jnp.zeros((depth, bp, W_FOX, lp), F32)
    kt_s = jnp.zeros((1, 1, W_FOX, n_s), F32)
    vt_s = jnp.zeros((1, 1, W_FOX, n_s), F32)
    for l in range(depth):
        (q, szf, qg, kg, vg, szg, la, sgf, sgg, kt_all, vt_all, ktb, vtb, lf) = _proj(
            xp, l, wts, PROJ_TILE, BF16, kt_all, vt_all, l)
        of = _fox(q, ktb, vtb, lf, FOX_TILE)
        ogz, s_new = _gla(l, qg, kg, la, vg, szg, zero_state, None, wts["g_gla_o"], GLA_CHUNK,
                          GLA_GROUP, GLA_TOKENS)
        flat = lambda a: a.reshape(bp * lp, a.shape[-1])
        xp = _out(l, flat(xp), flat(of), flat(szf), flat(ogz), flat(sgf), flat(sgg), pp, wts, OUT_TILE)
        xp = xp.reshape(bp, lp, D_MODEL)
        lfp_l.append(lf); sp_l.append(s_new)

        (q, szf, qg, kg, vg, szg, la, sgf, sgg, kt, vt, ktb, vtb, lf) = _proj(
            xs, l, wts, n_s, F32, kt_s, vt_s, 0)
        k_new = kt[0, 0].T.reshape(db, ls, W_FOX)
        v_new = vt[0, 0].T.reshape(db, ls, W_FOX)
        lf_new = jnp.transpose(lf[0].reshape(H_FOX, db, ls), (1, 0, 2))
        qbd = (q.reshape(db, ls, 1, W_FOX) * head_sel[None, None]).reshape(db, ls * H_FOX, W_FOX)
        rpad = (-ls) % 8
        of = _decode(l, page_table, qbd, cache_kt, cache_vt, cache_lft,
                     jnp.pad(k_new, ((0, 0), (0, rpad), (0, 0))), jnp.pad(v_new, ((0, 0), (0, rpad), (0, 0))),
                     jnp.pad(lf_new, ((0, 0), (0, 0), (0, PAGE_SIZE - ls))))
        seqpad = lambda a: jnp.pad(a.reshape(db, ls, a.shape[-1]), ((0, 0), (0, spad), (0, 0)))
        ogz, s_new = _gla(l, seqpad(qg), seqpad(kg), seqpad(la), seqpad(vg), seqpad(szg), state_gla, l,
                          wts["g_gla_o"], SAMPLE_CHUNK, GLA_GROUP, SAMPLE_CHUNK)
        ogz = ogz[:, :ls].reshape(n_s, WV_GLA)
        xs = _out(l, xs[0], of.reshape(n_s, W_FOX), szf[0], ogz, sgf[0], sgg[0], ps, wts, n_s)
        xs = xs.reshape(1, n_s, D_MODEL)
        ks_l.append(k_new.reshape(db, ls, H_FOX, HD_FOX)); vs_l.append(v_new.reshape(db, ls, H_FOX, HD_FOX))
        lfs_l.append(jnp.transpose(lf_new, (0, 2, 1))); ss_l.append(s_new)

    def heads_out(a):
        return jnp.transpose(a.reshape(depth, bp, H_FOX, HD_FOX, lp), (0, 1, 4, 2, 3))

    return (xp, xs.reshape(db, ls, D_MODEL), heads_out(kt_all), heads_out(vt_all),
            jnp.transpose(jnp.stack(lfp_l), (0, 1, 3, 2)), jnp.stack(sp_l),
            jnp.stack(ks_l), jnp.stack(vs_l), jnp.stack(lfs_l), jnp.stack(ss_l))
```

```python
import functools

import jax
import jax.numpy as jnp
from jax import lax
from jax.experimental import pallas as pl
from jax.experimental.pallas import tpu as pltpu

F32 = jnp.float32
BF16 = jnp.bfloat16

D_MODEL = 1024
PLE_DIM = 256
HD_FOX = 64
W_FOX = 512
H_FOX = 8
H_GLA = 4
WK_GLA = 256
WV_GLA = 512
DK_GLA = 64
DV_GLA = 128
GLA_RANK = 16
GLA_CHUNK = 64
PAGE_SIZE = 128
EPS = 1e-6
LANES = 128
LOG2E = 1.4426950408889634

_SIZES = (W_FOX, W_FOX, W_FOX, W_FOX, H_FOX, WK_GLA, WK_GLA, WV_GLA, WV_GLA, GLA_RANK, D_MODEL, D_MODEL)
_OFF = [0]
for _s in _SIZES:
    _OFF.append(_OFF[-1] + _s)
(_QF, _KF, _VF, _ZF, _FF, _QG, _KG, _VG, _ZG, _AG, _GF, _GG) = _OFF[:-1]

RANK_PAD = 128
_C_QF, _C_ZF, _C_QG, _C_KG, _C_VG, _C_ZG, _C_AG, _C_GF, _C_GG, N_STD = (
    0, 512, 1024, 1280, 1536, 2048, 2560, 2688, 3712, 4736)
N_TR = 1040

PAGES_PER_STEP = 16
DECODE_PIECE = 4
SAMPLE_CHUNK = 16
PROJ_TILE = 256
FOX_TILE = 512
FOX_ROWS = 256
FOX_HEADS = 4
OUT_TILE = 512
GLA_GROUP = 8
GLA_TOKENS = 256


def _log_sigmoid(x):
    return jnp.minimum(x, 0.0) - jnp.log1p(jnp.exp(-jnp.abs(x)))


def _silu(x):
    return x * jax.nn.sigmoid(x)


def _dot(a, b):
    return jnp.dot(a, b, preferred_element_type=F32)


def _dot_nt(a, b):
    return lax.dot_general(a, b, (((1,), (1,)), ((), ())), preferred_element_type=F32)


def _dot_tn(a, b):
    return lax.dot_general(a, b, (((0,), (0,)), ((), ())), preferred_element_type=F32)


def _cumsum_lanes(x):
    n = x.shape[-1]
    lane = lax.broadcasted_iota(jnp.int32, x.shape, x.ndim - 1)
    k = 1
    while k < n:
        x = x + jnp.where(lane >= k, pltpu.roll(x, k, axis=x.ndim - 1), 0.0)
        k *= 2
    return x


def _proj_kernel(x_ref, gpre_ref, wstd_ref, wtr_ref, bd_ref, gq_ref, gk_ref, wa2_ref, ba_ref, bff_ref,
                 bmf_ref, bmg_ref, kt_all_ref, vt_all_ref,
                 q_ref, szf_ref, qg_ref, kg_ref, vg_ref, szg_ref, la_ref, sgf_ref, sgg_ref,
                 kt_ref, vt_ref, ktb_ref, vtb_ref, lf_ref):
    del kt_all_ref, vt_all_ref
    x = x_ref[0]
    ta = x.shape[0]
    h = (x * lax.rsqrt(jnp.mean(x * x, axis=-1, keepdims=True) + EPS) * gpre_ref[0]).astype(BF16)

    def mm(lo, hi):
        return _dot(h, wstd_ref[0, :, lo:hi])

    qf = mm(_C_QF, _C_ZF)
    ss = _dot((qf * qf).astype(BF16), bd_ref[...]) * (1.0 / HD_FOX)
    q_ref[0] = (qf * lax.rsqrt(ss + EPS) * gq_ref[0]).astype(BF16)
    szf_ref[0] = _silu(mm(_C_ZF, _C_QG)).astype(BF16)
    qg_ref[0] = mm(_C_QG, _C_KG)
    kg_ref[0] = mm(_C_KG, _C_VG)
    vg_ref[0] = mm(_C_VG, _C_ZG).astype(vg_ref.dtype)
    szg_ref[0] = _silu(mm(_C_ZG, _C_AG)).astype(szg_ref.dtype)
    ag = mm(_C_AG, _C_GF)
    la_ref[0] = _log_sigmoid(_dot(ag.astype(BF16), wa2_ref[0]) + ba_ref[0]) * (1.0 / 16.0)
    sgf_ref[0] = jax.nn.sigmoid(mm(_C_GF, _C_GG) + bmf_ref[0]).astype(BF16)
    sgg_ref[0] = jax.nn.sigmoid(mm(_C_GG, N_STD) + bmg_ref[0]).astype(BF16)

    r = _dot_nt(wtr_ref[0], h)
    kt = r[0:W_FOX].reshape(H_FOX, HD_FOX, ta)
    kt = kt * lax.rsqrt(jnp.mean(kt * kt, axis=1, keepdims=True) + EPS) * gk_ref[0][None]
    kt = kt.reshape(W_FOX, ta)
    kt_ref[0, 0] = kt
    ktb_ref[0] = kt.astype(BF16)
    vt = r[W_FOX:2 * W_FOX]
    vt_ref[0, 0] = vt
    vtb_ref[0] = vt.astype(BF16)
    lf_ref[0] = _log_sigmoid(r[2 * W_FOX:2 * W_FOX + H_FOX] + bff_ref[0])


def _proj(x, l, wts, ta, gla_dtype, kt_all, vt_all, l_out):
    b_, l_, _ = x.shape
    grid = (b_, l_ // ta)
    tok = lambda w: pl.BlockSpec((1, ta, w), lambda b, t: (b, t, 0))
    feat = lambda r: pl.BlockSpec((1, r, ta), lambda b, t: (b, 0, t))
    slot = pl.BlockSpec((1, 1, W_FOX, ta), lambda b, t: (l_out, b, 0, t))
    lay = lambda *s: pl.BlockSpec((1,) + s, lambda b, t: (l,) + (0,) * len(s))
    const = lambda *s: pl.BlockSpec(s, lambda b, t: (0,) * len(s))
    hbm = pl.BlockSpec(memory_space=pl.ANY)
    sds = jax.ShapeDtypeStruct
    out_shape = (
        sds((b_, l_, W_FOX), BF16), sds((b_, l_, W_FOX), BF16),
        sds((b_, l_, WK_GLA), F32), sds((b_, l_, WK_GLA), F32),
        sds((b_, l_, WV_GLA), gla_dtype), sds((b_, l_, WV_GLA), gla_dtype),
        sds((b_, l_, WK_GLA), F32),
        sds((b_, l_, D_MODEL), BF16), sds((b_, l_, D_MODEL), BF16),
        sds(kt_all.shape, F32), sds(vt_all.shape, F32),
        sds((b_, W_FOX, l_), BF16), sds((b_, W_FOX, l_), BF16),
        sds((b_, H_FOX, l_), F32),
    )
    out_specs = (tok(W_FOX), tok(W_FOX), tok(WK_GLA), tok(WK_GLA), tok(WV_GLA), tok(WV_GLA), tok(WK_GLA),
                 tok(D_MODEL), tok(D_MODEL), slot, slot, feat(W_FOX), feat(W_FOX), feat(H_FOX))
    in_specs = [tok(D_MODEL), lay(1, D_MODEL), lay(D_MODEL, N_STD), lay(N_TR, D_MODEL), const(W_FOX, W_FOX),
                lay(1, W_FOX), lay(HD_FOX, 1), lay(RANK_PAD, WK_GLA), lay(1, WK_GLA), lay(H_FOX, 1),
                lay(1, D_MODEL), lay(1, D_MODEL), hbm, hbm]
    return pl.pallas_call(
        _proj_kernel, grid=grid, in_specs=in_specs, out_specs=out_specs, out_shape=out_shape,
        input_output_aliases={12: 9, 13: 10},
        compiler_params=pltpu.CompilerParams(dimension_semantics=("arbitrary", "arbitrary")),
        name="proj",
    )(x, wts["g_pre"], wts["w_std"], wts["w_tr"], wts["bd"], wts["g_q"], wts["g_k"], wts["w_a2"], wts["b_a"],
      wts["b_ff"], wts["b_mf"], wts["b_mg"], kt_all, vt_all)


def _fox_kernel(q_ref, kt_ref, vt_ref, lf_ref, o_ref, c_scr, s_scr, p_scr, *, tq):
    pr = pl.program_id(1)
    i = pl.program_id(2)
    n_heads = q_ref.shape[2] // HD_FOX

    @pl.when((pr == 0) & (i == 0))
    def _():
        c_scr[...] = _cumsum_lanes(lf_ref[0]) * LOG2E

    lane = lax.broadcasted_iota(jnp.int32, (1, 2 * HD_FOX), 1)
    pair_rows = [slice(2 * HD_FOX * (h // 2), 2 * HD_FOX * (h // 2 + 1)) for h in range(n_heads)]
    qh = []
    for h in range(n_heads):
        q_pair = q_ref[0, :, pair_rows[h]]
        qh.append(jnp.where((lane // HD_FOX) == h % 2, q_pair, jnp.zeros_like(q_pair)))
    rg = FOX_ROWS
    n_rg = tq // rg
    col = lax.broadcasted_iota(jnp.int32, (rg, tq), 1)
    rows = [lax.broadcasted_iota(jnp.int32, (rg, tq), 0) + r * rg for r in range(n_rg)]
    n_lane_tiles = tq // LANES

    def variant(n_blocks):
        crow = [c_scr[pl.ds(n_heads * pr + hh, 1), :] for hh in range(n_heads)]
        mx, m_b, lacc, acc = {}, {}, {}, {}

        def logits(hh, j, r):
            ks = slice(j * tq, (j + 1) * tq)
            rs = slice(r * rg, (r + 1) * rg)
            s = _dot(qh[hh][rs], kt_ref[0, pair_rows[hh], ks]) - crow[hh][:, ks]
            if j == n_blocks - 1:
                s = jnp.where(col <= rows[r], s, -jnp.inf)
            s_scr[hh, rs, ks] = s
            for c in range(n_lane_tiles):
                sc = s[:, c * LANES:(c + 1) * LANES]
                mx[hh, r] = sc if (hh, r) not in mx else jnp.maximum(mx[hh, r], sc)

        def row_max(hh):
            for r in range(n_rg):
                m_b[hh, r] = jnp.broadcast_to(jnp.max(mx[hh, r], axis=1, keepdims=True), (rg, LANES))

        def exps(hh, j, r):
            rs = slice(r * rg, (r + 1) * rg)
            for c in range(j * n_lane_tiles, (j + 1) * n_lane_tiles):
                cs = slice(c * LANES, (c + 1) * LANES)
                pc = jnp.exp2(s_scr[hh, rs, cs] - m_b[hh, r])
                lacc[hh, r] = pc if (hh, r) not in lacc else lacc[hh, r] + pc
                p_scr[hh, rs, cs] = pc.astype(BF16)

        def values(hh, j, r):
            ks = slice(j * tq, (j + 1) * tq)
            rs = slice(r * rg, (r + 1) * rg)
            pv = _dot_nt(p_scr[hh, rs, ks], vt_ref[0, pair_rows[hh], ks])
            acc[hh, r] = pv if (hh, r) not in acc else acc[hh, r] + pv

        items = [(j, r) for j in range(n_blocks) for r in range(n_rg)]
        for step in range(n_heads + 2):
            for j, r in items:
                if step < n_heads:
                    logits(step, j, r)
                if 0 <= step - 1 < n_heads:
                    exps(step - 1, j, r)
                if 0 <= step - 2 < n_heads:
                    values(step - 2, j, r)
            if step < n_heads:
                row_max(step)
        o = [jnp.concatenate([acc[hh, r] / jnp.sum(lacc[hh, r], axis=1, keepdims=True) for r in range(n_rg)],
                             axis=0) for hh in range(n_heads)]
        o_ref[0] = jnp.concatenate([jnp.where((lane // HD_FOX) == 0, o[h], o[h + 1])
                                    for h in range(0, n_heads, 2)], axis=1)

    for k in range(kt_ref.shape[2] // tq):
        @pl.when(i == k)
        def _(k=k):
            variant(k + 1)


def _fox(q, ktb, vtb, lf, tq):
    b_, l_, _ = q.shape
    nh = FOX_HEADS
    grid = (b_, H_FOX // nh, l_ // tq)
    return pl.pallas_call(
        functools.partial(_fox_kernel, tq=tq), grid=grid,
        in_specs=[pl.BlockSpec((1, tq, nh * HD_FOX), lambda b, p, i: (b, i, p)),
                  pl.BlockSpec((1, nh * HD_FOX, l_), lambda b, p, i: (b, p, 0)),
                  pl.BlockSpec((1, nh * HD_FOX, l_), lambda b, p, i: (b, p, 0)),
                  pl.BlockSpec((1, H_FOX, l_), lambda b, p, i: (b, 0, 0))],
        out_specs=pl.BlockSpec((1, tq, nh * HD_FOX), lambda b, p, i: (b, i, p)),
        out_shape=jax.ShapeDtypeStruct((b_, l_, W_FOX), F32),
        scratch_shapes=[pltpu.VMEM((H_FOX, l_), F32), pltpu.VMEM((nh, tq, l_), F32),
                        pltpu.VMEM((nh, tq, l_), BF16)],
        compiler_params=pltpu.CompilerParams(dimension_semantics=("arbitrary", "arbitrary", "arbitrary")),
        name="fox",
    )(q, ktb, vtb, lf)


def _decode_kernel(pk_ref, pv_ref, sk_ref, sv_ref, qbd_ref, qbd_prev_ref, *refs, n_steps, n_groups):
    del pk_ref, pv_ref, sk_ref, sv_ref
    n = PAGES_PER_STEP
    k_refs, v_refs, lf_refs = refs[0:n], refs[n:2 * n], refs[2 * n:3 * n]
    knew_ref, vnew_ref, lfnew_ref, o_ref, m_scr, l_scr, acc_scr, coff_scr, ctot_scr, s_scr = refs[3 * n:]
    t = pl.program_id(0)
    rows = qbd_ref.shape[1]
    n_q = rows // H_FOX
    slot_w = t % 2
    slot_r = 1 - slot_w
    first_of_seq = (t % n_steps) == 0

    @pl.when(t == 0)
    def _():
        coff_scr[...] = jnp.zeros(coff_scr.shape, F32)
        ctot_scr[...] = jnp.zeros(ctot_scr.shape, F32)
        s_scr[1] = jnp.full(s_scr.shape[1:], -jnp.inf, F32)

    @pl.when((t == 0) | ((t - 1) % n_steps == 0))
    def _():
        m_scr[...] = jnp.full(m_scr.shape, jnp.finfo(F32).min, F32)
        l_scr[...] = jnp.zeros(l_scr.shape, F32)
        acc_scr[...] = jnp.zeros(acc_scr.shape, F32)

    qbd = qbd_ref[0]

    def update(s, v_fn):
        m_prev = m_scr[...]
        m_new = jnp.maximum(m_prev, jnp.max(s, axis=1, keepdims=True))
        alpha = jnp.exp2(m_prev - m_new)
        p = jnp.exp2(s - m_new)
        l_scr[...] = alpha * l_scr[...] + jnp.sum(p, axis=1, keepdims=True)
        acc_scr[...] = alpha * acc_scr[...] + v_fn(p.astype(BF16))
        m_scr[...] = m_new

    s_prev = s_scr[slot_r]
    m_prev = m_scr[...]
    m_new = jnp.maximum(m_prev, jnp.max(s_prev, axis=1, keepdims=True))
    alpha = jnp.exp2(m_prev - m_new)
    p_f32 = jnp.exp2(s_prev - m_new)
    l_scr[...] = alpha * l_scr[...] + jnp.sum(p_f32, axis=1, keepdims=True)
    m_scr[...] = m_new
    p = p_f32.astype(BF16)

    lf = jnp.concatenate([r[0, 0] for r in lf_refs], axis=1)
    c = jnp.where(first_of_seq, 0.0, coff_scr[...]) + _cumsum_lanes(lf)
    c_end = c[:, n * PAGE_SIZE - 1:n * PAGE_SIZE]
    live = t < n_groups
    coff_scr[...] = jnp.where(live, c_end, coff_scr[...])
    ctot_scr[...] = jnp.where(live & ((t % n_steps) == n_steps - 1), c_end, ctot_scr[...])
    bias = jnp.concatenate([c * LOG2E] * n_q, axis=0)

    piece = DECODE_PIECE
    pv = None
    for g in range(n // piece):
        cols = slice(g * piece * PAGE_SIZE, (g + 1) * piece * PAGE_SIZE)
        pages = lambda refs: jnp.concatenate(
            [r[0, 0].reshape(W_FOX, PAGE_SIZE).astype(BF16) for r in refs[g * piece:(g + 1) * piece]], axis=1)
        s_scr[slot_w, :, cols] = _dot(qbd, pages(k_refs)) - bias[:, cols]
        d = _dot_nt(p[:, cols], pages(v_refs))
        pv = d if pv is None else pv + d
    acc_scr[...] = alpha * acc_scr[...] + pv

    @pl.when((t > 0) & first_of_seq)
    def _():
        qbd_prev = qbd_prev_ref[0]
        pad = jnp.zeros((PAGE_SIZE - knew_ref.shape[1], W_FOX), F32)
        knew = jnp.concatenate([knew_ref[0], pad], axis=0).astype(BF16)
        vnew = jnp.concatenate([vnew_ref[0], pad], axis=0).astype(BF16)
        cn = ctot_scr[...] + _cumsum_lanes(lfnew_ref[0])
        s2 = _dot_nt(qbd_prev, knew) - jnp.concatenate([cn * LOG2E] * n_q, axis=0)
        qi = lax.broadcasted_iota(jnp.int32, s2.shape, 0) // H_FOX
        ki = lax.broadcasted_iota(jnp.int32, s2.shape, 1)
        s2 = jnp.where(ki <= qi, s2, -jnp.inf)
        update(s2, lambda p: _dot(p, vnew))
        o = acc_scr[...] / l_scr[...]
        hrow = lax.broadcasted_iota(jnp.int32, (H_FOX, W_FOX), 0)
        hcol = lax.broadcasted_iota(jnp.int32, (H_FOX, W_FOX), 1) // HD_FOX
        sel = jnp.where(hrow == hcol, 1.0, 0.0)
        o_ref[0] = jnp.sum(o.reshape(n_q, H_FOX, W_FOX) * sel[None], axis=1)


def _decode(l, page_table, qbd, cache_kt, cache_vt, cache_lft, knew, vnew, lfnew):
    db, n_pages = page_table.shape
    n = PAGES_PER_STEP
    n_steps = n_pages // n
    rows = qbd.shape[1]
    n_q = rows // H_FOX

    assert n_steps >= 2
    n_groups = db * n_steps

    steps = jnp.arange(n_groups + 1, dtype=jnp.int32)
    groups = (jnp.minimum(steps, n_groups - 1), jnp.maximum(steps - 1, 0))
    flat_pages = page_table.reshape(-1)
    page_tabs = [flat_pages[(g[:, None] * n + jnp.arange(n, dtype=jnp.int32)[None, :]).reshape(-1)]
                 for g in groups]
    seq_tabs = [g // n_steps for g in groups]
    KEY, VAL = 0, 1

    def page_spec(i, shape, side):
        return pl.BlockSpec((1, 1) + shape,
                            lambda t, pk, pv, sk, sv: (l, (pk, pv)[side][t * n + i]) + (0,) * len(shape))

    def seq_spec(shape, side):
        return pl.BlockSpec((1,) + shape, lambda t, pk, pv, sk, sv: ((sk, sv)[side][t],) + (0,) * len(shape))

    in_specs = [seq_spec((rows, W_FOX), KEY), seq_spec((rows, W_FOX), VAL)]
    in_specs += [page_spec(i, (H_FOX, HD_FOX, PAGE_SIZE), KEY) for i in range(n)]
    in_specs += [page_spec(i, (H_FOX, HD_FOX, PAGE_SIZE), VAL) for i in range(n)]
    in_specs += [page_spec(i, (H_FOX, PAGE_SIZE), KEY) for i in range(n)]
    in_specs += [seq_spec((knew.shape[1], W_FOX), VAL), seq_spec((vnew.shape[1], W_FOX), VAL),
                 seq_spec((H_FOX, PAGE_SIZE), VAL)]
    grid_spec = pltpu.PrefetchScalarGridSpec(
        num_scalar_prefetch=4, grid=(n_groups + 1,), in_specs=in_specs,
        out_specs=seq_spec((n_q, W_FOX), VAL),
        scratch_shapes=[pltpu.VMEM((rows, 1), F32), pltpu.VMEM((rows, 1), F32), pltpu.VMEM((rows, W_FOX), F32),
                        pltpu.VMEM((H_FOX, 1), F32), pltpu.VMEM((H_FOX, 1), F32),
                        pltpu.VMEM((2, rows, n * PAGE_SIZE), F32)])
    return pl.pallas_call(
        functools.partial(_decode_kernel, n_steps=n_steps, n_groups=n_groups), grid_spec=grid_spec,
        out_shape=jax.ShapeDtypeStruct((db, n_q, W_FOX), F32),
        compiler_params=pltpu.CompilerParams(dimension_semantics=("arbitrary",)),
        name="decode",
    )(*page_tabs, *seq_tabs, qbd, qbd, *([cache_kt] * n), *([cache_vt] * n), *([cache_lft] * n), knew, vnew, lfnew)


def _gla_kernel(q_ref, k_ref, la_ref, v_ref, szg_ref, s0_ref, gno_ref, o_ref, sfin_ref, st_scr, *, chunk, n_chunks):
    c_ = chunk
    nb = q_ref.shape[0]
    t = pl.program_id(1)
    row = lax.broadcasted_iota(jnp.int32, (c_, c_), 0)
    col = lax.broadcasted_iota(jnp.int32, (c_, c_), 1)
    tril = col <= row
    tril_b = jnp.where(tril, 1.0, 0.0).astype(BF16)
    lane = lax.broadcasted_iota(jnp.int32, (1, 2 * DK_GLA), 1)
    srow = lax.broadcasted_iota(jnp.int32, (2 * DK_GLA, 2 * DV_GLA), 0) // DK_GLA
    scol = lax.broadcasted_iota(jnp.int32, (2 * DK_GLA, 2 * DV_GLA), 1) // DV_GLA
    blockdiag = srow == scol
    zero = jnp.zeros((DK_GLA, DV_GLA), F32)

    @pl.when(t == 0)
    def _():
        for bi in range(nb):
            for p in range(2):
                top = jnp.concatenate([s0_ref[bi, 2 * p], zero], axis=1)
                bot = jnp.concatenate([zero, s0_ref[bi, 2 * p + 1]], axis=1)
                st_scr[bi, p] = jnp.concatenate([top, bot], axis=0)

    pair_k = [slice(2 * DK_GLA * p, 2 * DK_GLA * (p + 1)) for p in range(2)]
    pair_v = [slice(2 * DV_GLA * p, 2 * DV_GLA * (p + 1)) for p in range(2)]
    units = [(bi, p) for bi in range(nb) for p in range(2)]

    def body(ci, carry):
        off = pl.multiple_of(ci * c_, c_)
        qd, kin, kout, v, bl = {}, {}, {}, {}, {}
        for bi in range(nb):
            la = la_ref[bi, pl.ds(off, c_), :]
            la_hi = la.astype(BF16)
            la_lo = (la - la_hi.astype(F32)).astype(BF16)
            bc = _dot(tril_b, la_hi) + _dot(tril_b, la_lo)
            bl[bi] = bc[c_ - 1:c_, :]
            q = q_ref[bi, pl.ds(off, c_), :]
            k = k_ref[bi, pl.ds(off, c_), :]
            qd[bi] = (q * jnp.exp(bc) * (DK_GLA ** -0.5)).astype(BF16)
            kin[bi] = (k * jnp.exp(-bc)).astype(BF16)
            kout[bi] = (k * jnp.exp(bl[bi] - bc)).astype(BF16)
            v[bi] = v_ref[bi, pl.ds(off, c_), :].astype(BF16)

        o_inter, raw = {}, {}
        for bi, p in units:
            qd_p, kin_p = qd[bi][:, pair_k[p]], kin[bi][:, pair_k[p]]
            o_inter[bi, p] = _dot(qd_p, st_scr[bi, p].astype(BF16))
            head_q = lambda hh: jnp.where((lane // DK_GLA) == hh, qd_p, jnp.zeros_like(qd_p))
            if c_ < GLA_CHUNK:
                a2 = _dot_nt(jnp.concatenate([head_q(0), head_q(1)], axis=0), kin_p)
                raw[bi, p] = [a2[hh * c_:(hh + 1) * c_] for hh in range(2)]
            else:
                raw[bi, p] = [_dot_nt(head_q(hh), kin_p) for hh in range(2)]

        outs = {bi: [] for bi in range(nb)}
        for bi, p in units:
            v_p = v[bi][:, pair_v[p]]
            for hh in range(2):
                a = jnp.where(tril, raw[bi, p][hh], 0.0)
                hs = slice(DV_GLA * hh, DV_GLA * (hh + 1))
                o_h = _dot(a.astype(BF16), v_p[:, hs]) + o_inter[bi, p][:, hs]
                o_h = o_h * lax.rsqrt(jnp.mean(o_h * o_h, axis=-1, keepdims=True) + EPS) * gno_ref[0]
                outs[bi].append(o_h)

        for bi, p in units:
            ds = _dot_tn(kout[bi][:, pair_k[p]], v[bi][:, pair_v[p]])
            dcol = jnp.exp(jnp.broadcast_to(bl[bi][:, pair_k[p]], (2 * DK_GLA, 2 * DK_GLA)).T)
            st_scr[bi, p] = (st_scr[bi, p] * jnp.concatenate([dcol, dcol], axis=1)
                             + jnp.where(blockdiag, ds, 0.0))

        for bi in range(nb):
            o = jnp.concatenate(outs[bi], axis=1) * szg_ref[bi, pl.ds(off, c_), :].astype(F32)
            o_ref[bi, pl.ds(off, c_), :] = o.astype(o_ref.dtype)
        return carry

    lax.fori_loop(0, n_chunks, body, 0)

    @pl.when(t == pl.num_programs(1) - 1)
    def _():
        for bi in range(nb):
            for p in range(2):
                s = st_scr[bi, p]
                sfin_ref[bi, 2 * p] = s[0:DK_GLA, 0:DV_GLA]
                sfin_ref[bi, 2 * p + 1] = s[DK_GLA:2 * DK_GLA, DV_GLA:2 * DV_GLA]


def _gla(l, q, k, la, v, szg, s0, s0_layer, gno, chunk, nb, lt):
    b_, l_, _ = q.shape
    nb = min(nb, b_)
    seq = lambda w: pl.BlockSpec((nb, lt, w), lambda b, t: (b, t, 0))
    if s0_layer is None:
        s0_spec = pl.BlockSpec((nb, H_GLA, DK_GLA, DV_GLA), lambda b, t: (b, 0, 0, 0))
    else:
        s0_spec = pl.BlockSpec((None, nb, H_GLA, DK_GLA, DV_GLA), lambda b, t: (s0_layer, b, 0, 0, 0))
    return pl.pallas_call(
        functools.partial(_gla_kernel, chunk=chunk, n_chunks=lt // chunk), grid=(b_ // nb, l_ // lt),
        in_specs=[seq(WK_GLA), seq(WK_GLA), seq(WK_GLA), seq(WV_GLA), seq(WV_GLA), s0_spec,
                  pl.BlockSpec((1, 1, DV_GLA), lambda b, t: (l, 0, 0))],
        out_specs=(seq(WV_GLA), pl.BlockSpec((nb, H_GLA, DK_GLA, DV_GLA), lambda b, t: (b, 0, 0, 0))),
        out_shape=(jax.ShapeDtypeStruct((b_, l_, WV_GLA), BF16),
                   jax.ShapeDtypeStruct((b_, H_GLA, DK_GLA, DV_GLA), F32)),
        scratch_shapes=[pltpu.VMEM((nb, 2, 2 * DK_GLA, 2 * DV_GLA), F32)],
        compiler_params=pltpu.CompilerParams(dimension_semantics=("arbitrary", "arbitrary")),
        name="gla",
    )(q, k, la, v, szg, s0, gno)


def _out_kernel(x_ref, of_ref, szf_ref, ogz_ref, sgf_ref, sgg_ref, p_ref,
                wbf_ref, wbg_ref, wout_ref, wpg_ref, wple_ref, gple_ref, bpg_ref, y_ref):
    x = x_ref[...]
    yf = _dot((of_ref[...] * szf_ref[...].astype(F32)).astype(BF16), wbf_ref[0])
    yg = _dot(ogz_ref[...], wbg_ref[0])
    m = sgf_ref[...].astype(F32) * yf + sgg_ref[...].astype(F32) * yg
    x1 = x + _dot(m.astype(BF16), wout_ref[0])
    hp = (x1 * lax.rsqrt(jnp.mean(x1 * x1, axis=-1, keepdims=True) + EPS) * gple_ref[0]).astype(BF16)
    gate = jax.nn.sigmoid(_dot(hp, wpg_ref[0]) + bpg_ref[0])
    y_ref[...] = x1 + gate * _dot(p_ref[0].astype(BF16), wple_ref[0])


def _out(l, x, of, szf, ogz, sgf, sgg, p_all, wts, td):
    n_tok = x.shape[0]
    tok = lambda w: pl.BlockSpec((td, w), lambda t: (t, 0))
    lay = lambda *s: pl.BlockSpec((1,) + s, lambda t: (l,) + (0,) * len(s))
    return pl.pallas_call(
        _out_kernel, grid=(n_tok // td,),
        in_specs=[tok(D_MODEL), tok(W_FOX), tok(W_FOX), tok(WV_GLA), tok(D_MODEL), tok(D_MODEL),
                  pl.BlockSpec((1, td, PLE_DIM), lambda t: (l, t, 0)),
                  lay(W_FOX, D_MODEL), lay(WV_GLA, D_MODEL), lay(D_MODEL, D_MODEL), lay(D_MODEL, D_MODEL),
                  lay(PLE_DIM, D_MODEL), lay(1, D_MODEL), lay(1, D_MODEL)],
        out_specs=tok(D_MODEL),
        out_shape=jax.ShapeDtypeStruct((n_tok, D_MODEL), F32),
        compiler_params=pltpu.CompilerParams(dimension_semantics=("arbitrary",)),
        name="out",
    )(x, of, szf, ogz, sgf, sgg, p_all, wts["w_br_fox"], wts["w_br_gla"], wts["w_out"], wts["w_ple_gate"],
      wts["w_ple"], wts["g_ple"], wts["b_ple_gate"])


def _prepare_weights(w_in, b_fox_f, g_pre, g_q, g_k, w_gla_a2, b_gla_a, g_gla_o, b_merge, w_br_fox, w_br_gla,
                     w_out, g_ple, w_ple_gate, b_ple_gate, w_ple):
    depth = w_in.shape[0]
    cols = lambda a, n: w_in[:, :, a:a + n].astype(BF16)
    a_pad = jnp.pad(cols(_AG, GLA_RANK), ((0, 0), (0, 0), (0, RANK_PAD - GLA_RANK)))
    w_std = jnp.concatenate([cols(_QF, W_FOX), cols(_ZF, W_FOX), cols(_QG, WK_GLA), cols(_KG, WK_GLA),
                             cols(_VG, WV_GLA), cols(_ZG, WV_GLA), a_pad, cols(_GF, D_MODEL), cols(_GG, D_MODEL)],
                            axis=-1)
    w_tr = jnp.concatenate([cols(_KF, W_FOX), cols(_VF, W_FOX), cols(_FF, H_FOX)], axis=-1)
    w_tr = jnp.pad(jnp.swapaxes(w_tr, 1, 2), ((0, 0), (0, N_TR - 2 * W_FOX - H_FOX), (0, 0)))
    head = jnp.arange(W_FOX) // HD_FOX
    return {
        "w_std": w_std, "w_tr": w_tr,
        "bd": (head[:, None] == head[None, :]).astype(BF16),
        "g_pre": g_pre[:, None, :],
        "g_q": jnp.tile(g_q, (1, H_FOX))[:, None, :] * (LOG2E * HD_FOX ** -0.5),
        "g_k": g_k[:, :, None],
        "w_a2": jnp.pad(w_gla_a2, ((0, 0), (0, RANK_PAD - GLA_RANK), (0, 0))).astype(BF16),
        "b_a": b_gla_a[:, None, :],
        "b_ff": b_fox_f[:, :, None],
        "b_mf": b_merge[:, 0:1, :], "b_mg": b_merge[:, 1:2, :],
        "g_gla_o": g_gla_o[:, None, :],
        "w_br_fox": w_br_fox.astype(BF16), "w_br_gla": w_br_gla.astype(BF16), "w_out": w_out.astype(BF16),
        "w_ple_gate": w_ple_gate.astype(BF16), "w_ple": w_ple.astype(BF16),
        "g_ple": g_ple[:, None, :], "b_ple_gate": b_ple_gate[:, None, :],
    }


def kernel(x_prompt, x_sample, p_prompt, p_sample, cache_k, cache_v, cache_logf, state_gla, page_table, w_in, b_fox_f, g_pre, g_q, g_k, w_gla_a2, b_gla_a, g_gla_o, b_merge, w_br_fox, w_br_gla, w_out, g_ple, w_ple_gate, b_ple_gate, w_ple):
    depth = w_in.shape[0]
    bp, lp, _ = x_prompt.shape
    db, ls, _ = x_sample.shape
    n_s = db * ls
    wts = _prepare_weights(w_in, b_fox_f, g_pre, g_q, g_k, w_gla_a2, b_gla_a, g_gla_o, b_merge, w_br_fox,
                           w_br_gla, w_out, g_ple, w_ple_gate, b_ple_gate, w_ple)
    cache_kt = jnp.transpose(cache_k, (0, 1, 3, 4, 2))
    cache_vt = jnp.transpose(cache_v, (0, 1, 3, 4, 2))
    cache_lft = jnp.transpose(cache_logf, (0, 1, 3, 2))
    pp = p_prompt.reshape(depth, bp * lp, PLE_DIM)
    ps = p_sample.reshape(depth, n_s, PLE_DIM)
    head_sel = (jnp.arange(H_FOX)[:, None] == (jnp.arange(W_FOX) // HD_FOX)[None, :]).astype(BF16)
    zero_state = jnp.zeros((bp, H_GLA, DK_GLA, DV_GLA), F32)
    spad = SAMPLE_CHUNK - ls

    xp = x_prompt
    xs = x_sample.reshape(1, n_s, D_MODEL)
    lfp_l, sp_l, ks_l, vs_l, lfs_l, ss_l = ([] for _ in range(6))
    kt_all = jnp.zeros((depth, bp, W_FOX, lp), F32)
    vt_all = jnp.zeros((depth, bp, W_FOX, lp), F32)
    kt_s = jnp.zeros((1, 1, W_FOX, n_s), F32)
    vt_s = jnp.zeros((1, 1, W_FOX, n_s), F32)
    for l in range(depth):
        (q, szf, qg, kg, vg, szg, la, sgf, sgg, kt_all, vt_all, ktb, vtb, lf) = _proj(
            xp, l, wts, PROJ_TILE, BF16, kt_all, vt_all, l)
        of = _fox(q, ktb, vtb, lf, FOX_TILE)
        ogz, s_new = _gla(l, qg, kg, la, vg, szg, zero_state, None, wts["g_gla_o"], GLA_CHUNK,
                          GLA_GROUP, GLA_TOKENS)
        flat = lambda a: a.reshape(bp * lp, a.shape[-1])
        xp = _out(l, flat(xp), flat(of), flat(szf), flat(ogz), flat(sgf), flat(sgg), pp, wts, OUT_TILE)
        xp = xp.reshape(bp, lp, D_MODEL)
        lfp_l.append(lf); sp_l.append(s_new)

        (q, szf, qg, kg, vg, szg, la, sgf, sgg, kt, vt, ktb, vtb, lf) = _proj(
            xs, l, wts, n_s, F32, kt_s, vt_s, 0)
        k_new = kt[0, 0].T.reshape(db, ls, W_FOX)
        v_new = vt[0, 0].T.reshape(db, ls, W_FOX)
        lf_new = jnp.transpose(lf[0].reshape(H_FOX, db, ls), (1, 0, 2))
        qbd = (q.reshape(db, ls, 1, W_FOX) * head_sel[None, None]).reshape(db, ls * H_FOX, W_FOX)
        rpad = (-ls) % 8
        of = _decode(l, page_table, qbd, cache_kt, cache_vt, cache_lft,
                     jnp.pad(k_new, ((0, 0), (0, rpad), (0, 0))), jnp.pad(v_new, ((0, 0), (0, rpad), (0, 0))),
                     jnp.pad(lf_new, ((0, 0), (0, 0), (0, PAGE_SIZE - ls))))
        seqpad = lambda a: jnp.pad(a.reshape(db, ls, a.shape[-1]), ((0, 0), (0, spad), (0, 0)))
        ogz, s_new = _gla(l, seqpad(qg), seqpad(kg), seqpad(la), seqpad(vg), seqpad(szg), state_gla, l,
                          wts["g_gla_o"], SAMPLE_CHUNK, GLA_GROUP, SAMPLE_CHUNK)
        ogz = ogz[:, :ls].reshape(n_s, WV_GLA)
        xs = _out(l, xs[0], of.reshape(n_s, W_FOX), szf[0], ogz, sgf[0], sgg[0], ps, wts, n_s)
        xs = xs.reshape(1, n_s, D_MODEL)
        ks_l.append(k_new.reshape(db, ls, H_FOX, HD_FOX)); vs_l.append(v_new.reshape(db, ls, H_FOX, HD_FOX))
        lfs_l.append(jnp.transpose(lf_new, (0, 2, 1))); ss_l.append(s_new)

    def heads_out(a):
        return jnp.transpose(a.reshape(depth, bp, H_FOX, HD_FOX, lp), (0, 1, 4, 2, 3))

    return (xp, xs.reshape(db, ls, D_MODEL), heads_out(kt_all), heads_out(vt_all),
            jnp.transpose(jnp.stack(lfp_l), (0, 1, 3, 2)), jnp.stack(sp_l),
            jnp.stack(ks_l), jnp.stack(vs_l), jnp.stack(lfs_l), jnp.stack(ss_l))
```

```python
import functools

import jax
import jax.numpy as jnp
from jax import lax
from jax.experimental import pallas as pl
from jax.experimental.pallas import tpu as pltpu

F32 = jnp.float32
BF16 = jnp.bfloat16

D_MODEL = 1024
PLE_DIM = 256
HD_FOX = 64
W_FOX = 512
H_FOX = 8
H_GLA = 4
WK_GLA = 256
WV_GLA = 512
DK_GLA = 64
DV_GLA = 128
GLA_RANK = 16
GLA_CHUNK = 64
PAGE_SIZE = 128
EPS = 1e-6
LANES = 128
LOG2E = 1.4426950408889634

_SIZES = (W_FOX, W_FOX, W_FOX, W_FOX, H_FOX, WK_GLA, WK_GLA, WV_GLA, WV_GLA, GLA_RANK, D_MODEL, D_MODEL)
_OFF = [0]
for _s in _SIZES:
    _OFF.append(_OFF[-1] + _s)
(_QF, _KF, _VF, _ZF, _FF, _QG, _KG, _VG, _ZG, _AG, _GF, _GG) = _OFF[:-1]

RANK_PAD = 128
_C_QF, _C_ZF, _C_QG, _C_KG, _C_VG, _C_ZG, _C_AG, _C_GF, _C_GG, N_STD = (
    0, 512, 1024, 1280, 1536, 2048, 2560, 2688, 3712, 4736)
N_TR = 1040

PAGES_PER_STEP = 16
DECODE_PIECE = 4
DECODE_SLOTS = 3
SAMPLE_CHUNK = 16
PROJ_TILE = 256
FOX_TILE = 512
FOX_ROWS = 256
FOX_HEADS = 4
OUT_TILE = 512
GLA_GROUP = 8
GLA_TOKENS = 256


def _log_sigmoid(x):
    return jnp.minimum(x, 0.0) - jnp.log1p(jnp.exp(-jnp.abs(x)))


def _silu(x):
    return x * jax.nn.sigmoid(x)


def _dot(a, b):
    return jnp.dot(a, b, preferred_element_type=F32)


def _dot_nt(a, b):
    return lax.dot_general(a, b, (((1,), (1,)), ((), ())), preferred_element_type=F32)


def _dot_tn(a, b):
    return lax.dot_general(a, b, (((0,), (0,)), ((), ())), preferred_element_type=F32)


def _cumsum_lanes(x):
    n = x.shape[-1]
    lane = lax.broadcasted_iota(jnp.int32, x.shape, x.ndim - 1)
    k = 1
    while k < n:
        x = x + jnp.where(lane >= k, pltpu.roll(x, k, axis=x.ndim - 1), 0.0)
        k *= 2
    return x


def _proj_kernel(x_ref, gpre_ref, wstd_ref, wtr_ref, bd_ref, gq_ref, gk_ref, wa2_ref, ba_ref, bff_ref,
                 bmf_ref, bmg_ref, kt_all_ref, vt_all_ref,
                 q_ref, szf_ref, qg_ref, kg_ref, vg_ref, szg_ref, la_ref, sgf_ref, sgg_ref,
                 kt_ref, vt_ref, ktb_ref, vtb_ref, lf_ref):
    del kt_all_ref, vt_all_ref
    x = x_ref[0]
    ta = x.shape[0]
    h = (x * lax.rsqrt(jnp.mean(x * x, axis=-1, keepdims=True) + EPS) * gpre_ref[0]).astype(BF16)

    def mm(lo, hi):
        return _dot(h, wstd_ref[0, :, lo:hi])

    qf = mm(_C_QF, _C_ZF)
    ss = _dot((qf * qf).astype(BF16), bd_ref[...]) * (1.0 / HD_FOX)
    q_ref[0] = (qf * lax.rsqrt(ss + EPS) * gq_ref[0]).astype(BF16)
    szf_ref[0] = _silu(mm(_C_ZF, _C_QG)).astype(BF16)
    qg_ref[0] = mm(_C_QG, _C_KG)
    kg_ref[0] = mm(_C_KG, _C_VG)
    vg_ref[0] = mm(_C_VG, _C_ZG).astype(vg_ref.dtype)
    szg_ref[0] = _silu(mm(_C_ZG, _C_AG)).astype(szg_ref.dtype)
    ag = mm(_C_AG, _C_GF)
    la_ref[0] = _log_sigmoid(_dot(ag.astype(BF16), wa2_ref[0]) + ba_ref[0]) * (1.0 / 16.0)
    sgf_ref[0] = jax.nn.sigmoid(mm(_C_GF, _C_GG) + bmf_ref[0]).astype(BF16)
    sgg_ref[0] = jax.nn.sigmoid(mm(_C_GG, N_STD) + bmg_ref[0]).astype(BF16)

    r = _dot_nt(wtr_ref[0], h)
    kt = r[0:W_FOX].reshape(H_FOX, HD_FOX, ta)
    kt = kt * lax.rsqrt(jnp.mean(kt * kt, axis=1, keepdims=True) + EPS) * gk_ref[0][None]
    kt = kt.reshape(W_FOX, ta)
    kt_ref[0, 0] = kt
    ktb_ref[0] = kt.astype(BF16)
    vt = r[W_FOX:2 * W_FOX]
    vt_ref[0, 0] = vt
    vtb_ref[0] = vt.astype(BF16)
    lf_ref[0] = _log_sigmoid(r[2 * W_FOX:2 * W_FOX + H_FOX] + bff_ref[0])


def _proj(x, l, wts, ta, gla_dtype, kt_all, vt_all, l_out):
    b_, l_, _ = x.shape
    grid = (b_, l_ // ta)
    tok = lambda w: pl.BlockSpec((1, ta, w), lambda b, t: (b, t, 0))
    feat = lambda r: pl.BlockSpec((1, r, ta), lambda b, t: (b, 0, t))
    slot = pl.BlockSpec((1, 1, W_FOX, ta), lambda b, t: (l_out, b, 0, t))
    lay = lambda *s: pl.BlockSpec((1,) + s, lambda b, t: (l,) + (0,) * len(s))
    const = lambda *s: pl.BlockSpec(s, lambda b, t: (0,) * len(s))
    hbm = pl.BlockSpec(memory_space=pl.ANY)
    sds = jax.ShapeDtypeStruct
    out_shape = (
        sds((b_, l_, W_FOX), BF16), sds((b_, l_, W_FOX), BF16),
        sds((b_, l_, WK_GLA), F32), sds((b_, l_, WK_GLA), F32),
        sds((b_, l_, WV_GLA), gla_dtype), sds((b_, l_, WV_GLA), gla_dtype),
        sds((b_, l_, WK_GLA), F32),
        sds((b_, l_, D_MODEL), BF16), sds((b_, l_, D_MODEL), BF16),
        sds(kt_all.shape, F32), sds(vt_all.shape, F32),
        sds((b_, W_FOX, l_), BF16), sds((b_, W_FOX, l_), BF16),
        sds((b_, H_FOX, l_), F32),
    )
    out_specs = (tok(W_FOX), tok(W_FOX), tok(WK_GLA), tok(WK_GLA), tok(WV_GLA), tok(WV_GLA), tok(WK_GLA),
                 tok(D_MODEL), tok(D_MODEL), slot, slot, feat(W_FOX), feat(W_FOX), feat(H_FOX))
    in_specs = [tok(D_MODEL), lay(1, D_MODEL), lay(D_MODEL, N_STD), lay(N_TR, D_MODEL), const(W_FOX, W_FOX),
                lay(1, W_FOX), lay(HD_FOX, 1), lay(RANK_PAD, WK_GLA), lay(1, WK_GLA), lay(H_FOX, 1),
                lay(1, D_MODEL), lay(1, D_MODEL), hbm, hbm]
    return pl.pallas_call(
        _proj_kernel, grid=grid, in_specs=in_specs, out_specs=out_specs, out_shape=out_shape,
        input_output_aliases={12: 9, 13: 10},
        compiler_params=pltpu.CompilerParams(dimension_semantics=("arbitrary", "arbitrary")),
        name="proj",
    )(x, wts["g_pre"], wts["w_std"], wts["w_tr"], wts["bd"], wts["g_q"], wts["g_k"], wts["w_a2"], wts["b_a"],
      wts["b_ff"], wts["b_mf"], wts["b_mg"], kt_all, vt_all)


def _fox_kernel(q_ref, kt_ref, vt_ref, lf_ref, o_ref, c_scr, s_scr, p_scr, *, tq):
    pr = pl.program_id(1)
    i = pl.program_id(2)
    n_heads = q_ref.shape[2] // HD_FOX

    @pl.when((pr == 0) & (i == 0))
    def _():
        c_scr[...] = _cumsum_lanes(lf_ref[0]) * LOG2E

    lane = lax.broadcasted_iota(jnp.int32, (1, 2 * HD_FOX), 1)
    pair_rows = [slice(2 * HD_FOX * (h // 2), 2 * HD_FOX * (h // 2 + 1)) for h in range(n_heads)]
    qh = []
    for h in range(n_heads):
        q_pair = q_ref[0, :, pair_rows[h]]
        qh.append(jnp.where((lane // HD_FOX) == h % 2, q_pair, jnp.zeros_like(q_pair)))
    rg = FOX_ROWS
    n_rg = tq // rg
    col = lax.broadcasted_iota(jnp.int32, (rg, tq), 1)
    rows = [lax.broadcasted_iota(jnp.int32, (rg, tq), 0) + r * rg for r in range(n_rg)]
    n_lane_tiles = tq // LANES

    def variant(n_blocks):
        crow = [c_scr[pl.ds(n_heads * pr + hh, 1), :] for hh in range(n_heads)]
        mx, m_b, lacc, acc = {}, {}, {}, {}

        def logits(hh, j, r):
            ks = slice(j * tq, (j + 1) * tq)
            rs = slice(r * rg, (r + 1) * rg)
            s = _dot(qh[hh][rs], kt_ref[0, pair_rows[hh], ks]) - crow[hh][:, ks]
            if j == n_blocks - 1:
                s = jnp.where(col <= rows[r], s, -jnp.inf)
            s_scr[hh, rs, ks] = s
            for c in range(n_lane_tiles):
                sc = s[:, c * LANES:(c + 1) * LANES]
                mx[hh, r] = sc if (hh, r) not in mx else jnp.maximum(mx[hh, r], sc)

        def row_max(hh):
            for r in range(n_rg):
                m_b[hh, r] = jnp.broadcast_to(jnp.max(mx[hh, r], axis=1, keepdims=True), (rg, LANES))

        def exps(hh, j, r):
            rs = slice(r * rg, (r + 1) * rg)
            for c in range(j * n_lane_tiles, (j + 1) * n_lane_tiles):
                cs = slice(c * LANES, (c + 1) * LANES)
                pc = jnp.exp2(s_scr[hh, rs, cs] - m_b[hh, r])
                lacc[hh, r] = pc if (hh, r) not in lacc else lacc[hh, r] + pc
                p_scr[hh, rs, cs] = pc.astype(BF16)

        def values(hh, j, r):
            ks = slice(j * tq, (j + 1) * tq)
            rs = slice(r * rg, (r + 1) * rg)
            pv = _dot_nt(p_scr[hh, rs, ks], vt_ref[0, pair_rows[hh], ks])
            acc[hh, r] = pv if (hh, r) not in acc else acc[hh, r] + pv

        items = [(j, r) for j in range(n_blocks) for r in range(n_rg)]
        for step in range(n_heads + 2):
            for j, r in items:
                if step < n_heads:
                    logits(step, j, r)
                if 0 <= step - 1 < n_heads:
                    exps(step - 1, j, r)
                if 0 <= step - 2 < n_heads:
                    values(step - 2, j, r)
            if step < n_heads:
                row_max(step)
        o = [jnp.concatenate([acc[hh, r] / jnp.sum(lacc[hh, r], axis=1, keepdims=True) for r in range(n_rg)],
                             axis=0) for hh in range(n_heads)]
        o_ref[0] = jnp.concatenate([jnp.where((lane // HD_FOX) == 0, o[h], o[h + 1])
                                    for h in range(0, n_heads, 2)], axis=1)

    for k in range(kt_ref.shape[2] // tq):
        @pl.when(i == k)
        def _(k=k):
            variant(k + 1)


def _fox(q, ktb, vtb, lf, tq):
    b_, l_, _ = q.shape
    nh = FOX_HEADS
    grid = (b_, H_FOX // nh, l_ // tq)
    return pl.pallas_call(
        functools.partial(_fox_kernel, tq=tq), grid=grid,
        in_specs=[pl.BlockSpec((1, tq, nh * HD_FOX), lambda b, p, i: (b, i, p)),
                  pl.BlockSpec((1, nh * HD_FOX, l_), lambda b, p, i: (b, p, 0)),
                  pl.BlockSpec((1, nh * HD_FOX, l_), lambda b, p, i: (b, p, 0)),
                  pl.BlockSpec((1, H_FOX, l_), lambda b, p, i: (b, 0, 0))],
        out_specs=pl.BlockSpec((1, tq, nh * HD_FOX), lambda b, p, i: (b, i, p)),
        out_shape=jax.ShapeDtypeStruct((b_, l_, W_FOX), F32),
        scratch_shapes=[pltpu.VMEM((H_FOX, l_), F32), pltpu.VMEM((nh, tq, l_), F32),
                        pltpu.VMEM((nh, tq, l_), BF16)],
        compiler_params=pltpu.CompilerParams(dimension_semantics=("arbitrary", "arbitrary", "arbitrary")),
        name="fox",
    )(q, ktb, vtb, lf)


def _decode_kernel(pg_ref, sk_ref, sv_ref, qbd_ref, qbd_prev_ref, kc_ref, vc_ref, lfc_ref, knew_ref, vnew_ref,
                   lfnew_ref, o_ref, m_scr, l_scr, acc_scr, coff_scr, ctot_scr, s_scr, kbuf, vbuf, lfbuf, sem,
                   *, n_steps, n_groups, layer):
    del sk_ref, sv_ref
    n = PAGES_PER_STEP
    t = pl.program_id(0)
    rows = qbd_ref.shape[1]
    n_q = rows // H_FOX
    slot_w = t % 2
    slot_r = 1 - slot_w
    first_of_seq = (t % n_steps) == 0
    K_SEM, V_SEM, LF_SEM = 0, 1, 2

    def page_copy(kind, g, i):
        src, dst, row = ((kc_ref, kbuf, K_SEM), (vc_ref, vbuf, V_SEM), (lfc_ref, lfbuf, LF_SEM))[kind]
        slot = g % DECODE_SLOTS
        return pltpu.make_async_copy(src.at[layer, pg_ref[g * n + i]], dst.at[slot, i], sem.at[row, slot])

    def start_group(kinds, g):
        for i in range(n):
            for kind in kinds:
                page_copy(kind, g, i).start()

    def wait_group(kinds, g):
        for i in range(n):
            for kind in kinds:
                page_copy(kind, g, i).wait()

    @pl.when(t == 0)
    def _():
        coff_scr[...] = jnp.zeros(coff_scr.shape, F32)
        ctot_scr[...] = jnp.zeros(ctot_scr.shape, F32)
        s_scr[1] = jnp.full(s_scr.shape[1:], -jnp.inf, F32)
        vbuf[DECODE_SLOTS - 1] = jnp.zeros(vbuf.shape[1:], F32)
        start_group((K_SEM, LF_SEM), 0)
        start_group((K_SEM, LF_SEM), 1)

    @pl.when(t + 2 < n_groups)
    def _():
        start_group((K_SEM, LF_SEM), t + 2)

    @pl.when(t < n_groups)
    def _():
        start_group((V_SEM,), t)
        wait_group((K_SEM, LF_SEM), t)

    @pl.when(t > 0)
    def _():
        wait_group((V_SEM,), t - 1)

    kslot = jnp.minimum(t, n_groups - 1) % DECODE_SLOTS
    vslot = (t + DECODE_SLOTS - 1) % DECODE_SLOTS
    k_refs = [kbuf.at[kslot, i] for i in range(n)]
    v_refs = [vbuf.at[vslot, i] for i in range(n)]
    lf_refs = [lfbuf.at[kslot, i] for i in range(n)]

    @pl.when((t == 0) | ((t - 1) % n_steps == 0))
    def _():
        m_scr[...] = jnp.full(m_scr.shape, jnp.finfo(F32).min, F32)
        l_scr[...] = jnp.zeros(l_scr.shape, F32)
        acc_scr[...] = jnp.zeros(acc_scr.shape, F32)

    qbd = qbd_ref[0]

    def update(s, v_fn):
        m_prev = m_scr[...]
        m_new = jnp.maximum(m_prev, jnp.max(s, axis=1, keepdims=True))
        alpha = jnp.exp2(m_prev - m_new)
        p = jnp.exp2(s - m_new)
        l_scr[...] = alpha * l_scr[...] + jnp.sum(p, axis=1, keepdims=True)
        acc_scr[...] = alpha * acc_scr[...] + v_fn(p.astype(BF16))
        m_scr[...] = m_new

    s_prev = s_scr[slot_r]
    m_prev = m_scr[...]
    m_new = jnp.maximum(m_prev, jnp.max(s_prev, axis=1, keepdims=True))
    alpha = jnp.exp2(m_prev - m_new)
    p_f32 = jnp.exp2(s_prev - m_new)
    l_scr[...] = alpha * l_scr[...] + jnp.sum(p_f32, axis=1, keepdims=True)
    m_scr[...] = m_new
    p = p_f32.astype(BF16)

    lf = jnp.concatenate([r[...] for r in lf_refs], axis=1)
    c = jnp.where(first_of_seq, 0.0, coff_scr[...]) + _cumsum_lanes(lf)
    c_end = c[:, n * PAGE_SIZE - 1:n * PAGE_SIZE]
    live = t < n_groups
    coff_scr[...] = jnp.where(live, c_end, coff_scr[...])
    ctot_scr[...] = jnp.where(live & ((t % n_steps) == n_steps - 1), c_end, ctot_scr[...])
    bias = jnp.concatenate([c * LOG2E] * n_q, axis=0)

    piece = DECODE_PIECE
    pv = None
    for g in range(n // piece):
        cols = slice(g * piece * PAGE_SIZE, (g + 1) * piece * PAGE_SIZE)
        pages = lambda refs: jnp.concatenate(
            [r[...].reshape(W_FOX, PAGE_SIZE).astype(BF16) for r in refs[g * piece:(g + 1) * piece]], axis=1)
        s_scr[slot_w, :, cols] = _dot(qbd, pages(k_refs)) - bias[:, cols]
        d = _dot_nt(p[:, cols], pages(v_refs))
        pv = d if pv is None else pv + d
    acc_scr[...] = alpha * acc_scr[...] + pv

    @pl.when((t > 0) & first_of_seq)
    def _():
        qbd_prev = qbd_prev_ref[0]
        pad = jnp.zeros((PAGE_SIZE - knew_ref.shape[1], W_FOX), F32)
        knew = jnp.concatenate([knew_ref[0], pad], axis=0).astype(BF16)
        vnew = jnp.concatenate([vnew_ref[0], pad], axis=0).astype(BF16)
        cn = ctot_scr[...] + _cumsum_lanes(lfnew_ref[0])
        s2 = _dot_nt(qbd_prev, knew) - jnp.concatenate([cn * LOG2E] * n_q, axis=0)
        qi = lax.broadcasted_iota(jnp.int32, s2.shape, 0) // H_FOX
        ki = lax.broadcasted_iota(jnp.int32, s2.shape, 1)
        s2 = jnp.where(ki <= qi, s2, -jnp.inf)
        update(s2, lambda p: _dot(p, vnew))
        o = acc_scr[...] / l_scr[...]
        hrow = lax.broadcasted_iota(jnp.int32, (H_FOX, W_FOX), 0)
        hcol = lax.broadcasted_iota(jnp.int32, (H_FOX, W_FOX), 1) // HD_FOX
        sel = jnp.where(hrow == hcol, 1.0, 0.0)
        o_ref[0] = jnp.sum(o.reshape(n_q, H_FOX, W_FOX) * sel[None], axis=1)


def _decode(l, page_table, qbd, cache_kt, cache_vt, cache_lft, knew, vnew, lfnew):
    db, n_pages = page_table.shape
    n = PAGES_PER_STEP
    n_steps = n_pages // n
    rows = qbd.shape[1]
    n_q = rows // H_FOX

    assert n_steps >= 2
    n_groups = db * n_steps

    assert n_groups >= DECODE_SLOTS
    steps = jnp.arange(n_groups + 1, dtype=jnp.int32)
    seq_tabs = [jnp.minimum(steps, n_groups - 1) // n_steps, jnp.maximum(steps - 1, 0) // n_steps]
    flat_pages = page_table.reshape(-1)
    KEY, VAL = 0, 1

    def seq_spec(shape, side):
        return pl.BlockSpec((1,) + shape, lambda t, pg, sk, sv: ((sk, sv)[side][t],) + (0,) * len(shape))

    hbm = pl.BlockSpec(memory_space=pl.ANY)
    in_specs = [seq_spec((rows, W_FOX), KEY), seq_spec((rows, W_FOX), VAL), hbm, hbm, hbm,
                seq_spec((knew.shape[1], W_FOX), VAL), seq_spec((vnew.shape[1], W_FOX), VAL),
                seq_spec((H_FOX, PAGE_SIZE), VAL)]
    grid_spec = pltpu.PrefetchScalarGridSpec(
        num_scalar_prefetch=3, grid=(n_groups + 1,), in_specs=in_specs,
        out_specs=seq_spec((n_q, W_FOX), VAL),
        scratch_shapes=[pltpu.VMEM((rows, 1), F32), pltpu.VMEM((rows, 1), F32), pltpu.VMEM((rows, W_FOX), F32),
                        pltpu.VMEM((H_FOX, 1), F32), pltpu.VMEM((H_FOX, 1), F32),
                        pltpu.VMEM((2, rows, n * PAGE_SIZE), F32),
                        pltpu.VMEM((DECODE_SLOTS, n, H_FOX, HD_FOX, PAGE_SIZE), F32),
                        pltpu.VMEM((DECODE_SLOTS, n, H_FOX, HD_FOX, PAGE_SIZE), F32),
                        pltpu.VMEM((DECODE_SLOTS, n, H_FOX, PAGE_SIZE), F32),
                        pltpu.SemaphoreType.DMA((3, DECODE_SLOTS))])
    return pl.pallas_call(
        functools.partial(_decode_kernel, n_steps=n_steps, n_groups=n_groups, layer=l), grid_spec=grid_spec,
        out_shape=jax.ShapeDtypeStruct((db, n_q, W_FOX), F32),
        compiler_params=pltpu.CompilerParams(dimension_semantics=("arbitrary",)),
        name="decode",
    )(flat_pages, *seq_tabs, qbd, qbd, cache_kt, cache_vt, cache_lft, knew, vnew, lfnew)


def _gla_kernel(q_ref, k_ref, la_ref, v_ref, szg_ref, s0_ref, gno_ref, o_ref, sfin_ref, st_scr, *, chunk, n_chunks):
    c_ = chunk
    nb = q_ref.shape[0]
    t = pl.program_id(1)
    row = lax.broadcasted_iota(jnp.int32, (c_, c_), 0)
    col = lax.broadcasted_iota(jnp.int32, (c_, c_), 1)
    tril = col <= row
    tril_b = jnp.where(tril, 1.0, 0.0).astype(BF16)
    lane = lax.broadcasted_iota(jnp.int32, (1, 2 * DK_GLA), 1)
    srow = lax.broadcasted_iota(jnp.int32, (2 * DK_GLA, 2 * DV_GLA), 0) // DK_GLA
    scol = lax.broadcasted_iota(jnp.int32, (2 * DK_GLA, 2 * DV_GLA), 1) // DV_GLA
    blockdiag = srow == scol
    zero = jnp.zeros((DK_GLA, DV_GLA), F32)

    @pl.when(t == 0)
    def _():
        for bi in range(nb):
            for p in range(2):
                top = jnp.concatenate([s0_ref[bi, 2 * p], zero], axis=1)
                bot = jnp.concatenate([zero, s0_ref[bi, 2 * p + 1]], axis=1)
                st_scr[bi, p] = jnp.concatenate([top, bot], axis=0)

    pair_k = [slice(2 * DK_GLA * p, 2 * DK_GLA * (p + 1)) for p in range(2)]
    pair_v = [slice(2 * DV_GLA * p, 2 * DV_GLA * (p + 1)) for p in range(2)]
    units = [(bi, p) for bi in range(nb) for p in range(2)]

    def body(ci, carry):
        off = pl.multiple_of(ci * c_, c_)
        qd, kin, kout, v, bl = {}, {}, {}, {}, {}
        for bi in range(nb):
            la = la_ref[bi, pl.ds(off, c_), :]
            la_hi = la.astype(BF16)
            la_lo = (la - la_hi.astype(F32)).astype(BF16)
            bc = _dot(tril_b, la_hi) + _dot(tril_b, la_lo)
            bl[bi] = bc[c_ - 1:c_, :]
            q = q_ref[bi, pl.ds(off, c_), :]
            k = k_ref[bi, pl.ds(off, c_), :]
            qd[bi] = (q * jnp.exp(bc) * (DK_GLA ** -0.5)).astype(BF16)
            kin[bi] = (k * jnp.exp(-bc)).astype(BF16)
            kout[bi] = (k * jnp.exp(bl[bi] - bc)).astype(BF16)
            v[bi] = v_ref[bi, pl.ds(off, c_), :].astype(BF16)

        o_inter, raw = {}, {}
        for bi, p in units:
            qd_p, kin_p = qd[bi][:, pair_k[p]], kin[bi][:, pair_k[p]]
            o_inter[bi, p] = _dot(qd_p, st_scr[bi, p].astype(BF16))
            head_q = lambda hh: jnp.where((lane // DK_GLA) == hh, qd_p, jnp.zeros_like(qd_p))
            if c_ < GLA_CHUNK:
                a2 = _dot_nt(jnp.concatenate([head_q(0), head_q(1)], axis=0), kin_p)
                raw[bi, p] = [a2[hh * c_:(hh + 1) * c_] for hh in range(2)]
            else:
                raw[bi, p] = [_dot_nt(head_q(hh), kin_p) for hh in range(2)]

        outs = {bi: [] for bi in range(nb)}
        for bi, p in units:
            v_p = v[bi][:, pair_v[p]]
            for hh in range(2):
                a = jnp.where(tril, raw[bi, p][hh], 0.0)
                hs = slice(DV_GLA * hh, DV_GLA * (hh + 1))
                o_h = _dot(a.astype(BF16), v_p[:, hs]) + o_inter[bi, p][:, hs]
                o_h = o_h * lax.rsqrt(jnp.mean(o_h * o_h, axis=-1, keepdims=True) + EPS) * gno_ref[0]
                outs[bi].append(o_h)

        for bi, p in units:
            ds = _dot_tn(kout[bi][:, pair_k[p]], v[bi][:, pair_v[p]])
            dcol = jnp.exp(jnp.broadcast_to(bl[bi][:, pair_k[p]], (2 * DK_GLA, 2 * DK_GLA)).T)
            st_scr[bi, p] = (st_scr[bi, p] * jnp.concatenate([dcol, dcol], axis=1)
                             + jnp.where(blockdiag, ds, 0.0))

        for bi in range(nb):
            o = jnp.concatenate(outs[bi], axis=1) * szg_ref[bi, pl.ds(off, c_), :].astype(F32)
            o_ref[bi, pl.ds(off, c_), :] = o.astype(o_ref.dtype)
        return carry

    lax.fori_loop(0, n_chunks, body, 0)

    @pl.when(t == pl.num_programs(1) - 1)
    def _():
        for bi in range(nb):
            for p in range(2):
                s = st_scr[bi, p]
                sfin_ref[bi, 2 * p] = s[0:DK_GLA, 0:DV_GLA]
                sfin_ref[bi, 2 * p + 1] = s[DK_GLA:2 * DK_GLA, DV_GLA:2 * DV_GLA]


def _gla(l, q, k, la, v, szg, s0, s0_layer, gno, chunk, nb, lt):
    b_, l_, _ = q.shape
    nb = min(nb, b_)
    seq = lambda w: pl.BlockSpec((nb, lt, w), lambda b, t: (b, t, 0))
    if s0_layer is None:
        s0_spec = pl.BlockSpec((nb, H_GLA, DK_GLA, DV_GLA), lambda b, t: (b, 0, 0, 0))
    else:
        s0_spec = pl.BlockSpec((None, nb, H_GLA, DK_GLA, DV_GLA), lambda b, t: (s0_layer, b, 0, 0, 0))
    return pl.pallas_call(
        functools.partial(_gla_kernel, chunk=chunk, n_chunks=lt // chunk), grid=(b_ // nb, l_ // lt),
        in_specs=[seq(WK_GLA), seq(WK_GLA), seq(WK_GLA), seq(WV_GLA), seq(WV_GLA), s0_spec,
                  pl.BlockSpec((1, 1, DV_GLA), lambda b, t: (l, 0, 0))],
        out_specs=(seq(WV_GLA), pl.BlockSpec((nb, H_GLA, DK_GLA, DV_GLA), lambda b, t: (b, 0, 0, 0))),
        out_shape=(jax.ShapeDtypeStruct((b_, l_, WV_GLA), BF16),
                   jax.ShapeDtypeStruct((b_, H_GLA, DK_GLA, DV_GLA), F32)),
        scratch_shapes=[pltpu.VMEM((nb, 2, 2 * DK_GLA, 2 * DV_GLA), F32)],
        compiler_params=pltpu.CompilerParams(dimension_semantics=("arbitrary", "arbitrary")),
        name="gla",
    )(q, k, la, v, szg, s0, gno)


def _out_kernel(x_ref, of_ref, szf_ref, ogz_ref, sgf_ref, sgg_ref, p_ref,
                wbf_ref, wbg_ref, wout_ref, wpg_ref, wple_ref, gple_ref, bpg_ref, y_ref):
    x = x_ref[...]
    yf = _dot((of_ref[...] * szf_ref[...].astype(F32)).astype(BF16), wbf_ref[0])
    yg = _dot(ogz_ref[...], wbg_ref[0])
    m = sgf_ref[...].astype(F32) * yf + sgg_ref[...].astype(F32) * yg
    x1 = x + _dot(m.astype(BF16), wout_ref[0])
    hp = (x1 * lax.rsqrt(jnp.mean(x1 * x1, axis=-1, keepdims=True) + EPS) * gple_ref[0]).astype(BF16)
    gate = jax.nn.sigmoid(_dot(hp, wpg_ref[0]) + bpg_ref[0])
    y_ref[...] = x1 + gate * _dot(p_ref[0].astype(BF16), wple_ref[0])


def _out(l, x, of, szf, ogz, sgf, sgg, p_all, wts, td):
    n_tok = x.shape[0]
    tok = lambda w: pl.BlockSpec((td, w), lambda t: (t, 0))
    lay = lambda *s: pl.BlockSpec((1,) + s, lambda t: (l,) + (0,) * len(s))
    return pl.pallas_call(
        _out_kernel, grid=(n_tok // td,),
        in_specs=[tok(D_MODEL), tok(W_FOX), tok(W_FOX), tok(WV_GLA), tok(D_MODEL), tok(D_MODEL),
                  pl.BlockSpec((1, td, PLE_DIM), lambda t: (l, t, 0)),
                  lay(W_FOX, D_MODEL), lay(WV_GLA, D_MODEL), lay(D_MODEL, D_MODEL), lay(D_MODEL, D_MODEL),
                  lay(PLE_DIM, D_MODEL), lay(1, D_MODEL), lay(1, D_MODEL)],
        out_specs=tok(D_MODEL),
        out_shape=jax.ShapeDtypeStruct((n_tok, D_MODEL), F32),
        compiler_params=pltpu.CompilerParams(dimension_semantics=("arbitrary",)),
        name="out",
    )(x, of, szf, ogz, sgf, sgg, p_all, wts["w_br_fox"], wts["w_br_gla"], wts["w_out"], wts["w_ple_gate"],
      wts["w_ple"], wts["g_ple"], wts["b_ple_gate"])


def _prepare_weights(w_in, b_fox_f, g_pre, g_q, g_k, w_gla_a2, b_gla_a, g_gla_o, b_merge, w_br_fox, w_br_gla,
                     w_out, g_ple, w_ple_gate, b_ple_gate, w_ple):
    depth = w_in.shape[0]
    cols = lambda a, n: w_in[:, :, a:a + n]
    a_pad = jnp.pad(cols(_AG, GLA_RANK), ((0, 0), (0, 0), (0, RANK_PAD - GLA_RANK)))
    w_std = jnp.concatenate([cols(_QF, W_FOX), cols(_ZF, W_FOX), cols(_QG, WK_GLA), cols(_KG, WK_GLA),
                             cols(_VG, WV_GLA), cols(_ZG, WV_GLA), a_pad, cols(_GF, D_MODEL), cols(_GG, D_MODEL)],
                            axis=-1).astype(BF16)
    w_tr = jnp.concatenate([cols(_KF, W_FOX), cols(_VF, W_FOX), cols(_FF, H_FOX)], axis=-1)
    w_tr = jnp.pad(jnp.swapaxes(w_tr, 1, 2), ((0, 0), (0, N_TR - 2 * W_FOX - H_FOX), (0, 0))).astype(BF16)
    head = jnp.arange(W_FOX) // HD_FOX
    return {
        "w_std": w_std, "w_tr": w_tr,
        "bd": (head[:, None] == head[None, :]).astype(BF16),
        "g_pre": g_pre[:, None, :],
        "g_q": jnp.tile(g_q, (1, H_FOX))[:, None, :] * (LOG2E * HD_FOX ** -0.5),
        "g_k": g_k[:, :, None],
        "w_a2": jnp.pad(w_gla_a2, ((0, 0), (0, RANK_PAD - GLA_RANK), (0, 0))).astype(BF16),
        "b_a": b_gla_a[:, None, :],
        "b_ff": b_fox_f[:, :, None],
        "b_mf": b_merge[:, 0:1, :], "b_mg": b_merge[:, 1:2, :],
        "g_gla_o": g_gla_o[:, None, :],
        "w_br_fox": w_br_fox.astype(BF16), "w_br_gla": w_br_gla.astype(BF16), "w_out": w_out.astype(BF16),
        "w_ple_gate": w_ple_gate.astype(BF16), "w_ple": w_ple.astype(BF16),
        "g_ple": g_ple[:, None, :], "b_ple_gate": b_ple_gate[:, None, :],
    }


def kernel(x_prompt, x_sample, p_prompt, p_sample, cache_k, cache_v, cache_logf, state_gla, page_table, w_in, b_fox_f, g_pre, g_q, g_k, w_gla_a2, b_gla_a, g_gla_o, b_merge, w_br_fox, w_br_gla, w_out, g_ple, w_ple_gate, b_ple_gate, w_ple):
    depth = w_in.shape[0]
    bp, lp, _ = x_prompt.shape
    db, ls, _ = x_sample.shape
    n_s = db * ls
    wts = _prepare_weights(w_in, b_fox_f, g_pre, g_q, g_k, w_gla_a2, b_gla_a, g_gla_o, b_merge, w_br_fox,
                           w_br_gla, w_out, g_ple, w_ple_gate, b_ple_gate, w_ple)
    cache_kt = jnp.transpose(cache_k, (0, 1, 3, 4, 2))
    cache_vt = jnp.transpose(cache_v, (0, 1, 3, 4, 2))
    cache_lft = jnp.transpose(cache_logf, (0, 1, 3, 2))
    pp = p_prompt.reshape(depth, bp * lp, PLE_DIM)
    ps = p_sample.reshape(depth, n_s, PLE_DIM)
    head_sel = (jnp.arange(H_FOX)[:, None] == (jnp.arange(W_FOX) // HD_FOX)[None, :]).astype(BF16)
    zero_state = jnp.zeros((bp, H_GLA, DK_GLA, DV_GLA), F32)
    spad = SAMPLE_CHUNK - ls

    xp = x_prompt
    xs = x_sample.reshape(1, n_s, D_MODEL)
    lfp_l, sp_l, ks_l, vs_l, lfs_l, ss_l = ([] for _ in range(6))
    kt_all = jnp.zeros((depth, bp, W_FOX, lp), F32)
    vt_all = jnp.zeros((depth, bp, W_FOX, lp), F32)
    kt_s = jnp.zeros((1, 1, W_FOX, n_s), F32)
    vt_s = jnp.zeros((1, 1, W_FOX, n_s), F32)
    for l in range(depth):
        (q, szf, qg, kg, vg, szg, la, sgf, sgg, kt_all, vt_all, ktb, vtb, lf) = _proj(
            xp, l, wts, PROJ_TILE, BF16, kt_all, vt_all, l)
        of = _fox(q, ktb, vtb, lf, FOX_TILE)
        ogz, s_new = _gla(l, qg, kg, la, vg, szg, zero_state, None, wts["g_gla_o"], GLA_CHUNK,
                          GLA_GROUP, GLA_TOKENS)
        flat = lambda a: a.reshape(bp * lp, a.shape[-1])
        xp = _out(l, flat(xp), flat(of), flat(szf), flat(ogz), flat(sgf), flat(sgg), pp, wts, OUT_TILE)
        xp = xp.reshape(bp, lp, D_MODEL)
        lfp_l.append(lf); sp_l.append(s_new)

        (q, szf, qg, kg, vg, szg, la, sgf, sgg, kt, vt, ktb, vtb, lf) = _proj(
            xs, l, wts, n_s, F32, kt_s, vt_s, 0)
        k_new = kt[0, 0].T.reshape(db, ls, W_FOX)
        v_new = vt[0, 0].T.reshape(db, ls, W_FOX)
        lf_new = jnp.transpose(lf[0].reshape(H_FOX, db, ls), (1, 0, 2))
        qbd = (q.reshape(db, ls, 1, W_FOX) * head_sel[None, None]).reshape(db, ls * H_FOX, W_FOX)
        rpad = (-ls) % 8
        of = _decode(l, page_table, qbd, cache_kt, cache_vt, cache_lft,
                     jnp.pad(k_new, ((0, 0), (0, rpad), (0, 0))), jnp.pad(v_new, ((0, 0), (0, rpad), (0, 0))),
                     jnp.pad(lf_new, ((0, 0), (0, 0), (0, PAGE_SIZE - ls))))
        seqpad = lambda a: jnp.pad(a.reshape(db, ls, a.shape[-1]), ((0, 0), (0, spad), (0, 0)))
        ogz, s_new = _gla(l, seqpad(qg), seqpad(kg), seqpad(la), seqpad(vg), seqpad(szg), state_gla, l,
                          wts["g_gla_o"], SAMPLE_CHUNK, GLA_GROUP, SAMPLE_CHUNK)
        ogz = ogz[:, :ls].reshape(n_s, WV_GLA)
        xs = _out(l, xs[0], of.reshape(n_s, W_FOX), szf[0], ogz, sgf[0], sgg[0], ps, wts, n_s)
        xs = xs.reshape(1, n_s, D_MODEL)
        ks_l.append(k_new.reshape(db, ls, H_FOX, HD_FOX)); vs_l.append(v_new.reshape(db, ls, H_FOX, HD_FOX))
        lfs_l.append(jnp.transpose(lf_new, (0, 2, 1))); ss_l.append(s_new)

    def heads_out(a):
        return jnp.transpose(a.reshape(depth, bp, H_FOX, HD_FOX, lp), (0, 1, 4, 2, 3))

    return (xp, xs.reshape(db, ls, D_MODEL), heads_out(kt_all), heads_out(vt_all),
            jnp.transpose(jnp.stack(lfp_l), (0, 1, 3, 2)), jnp.stack(sp_l),
            jnp.stack(ks_l), jnp.stack(vs_l), jnp.stack(lfs_l), jnp.stack(ss_l))
```
